```python
import math
import jax
import jax.numpy as jnp
from jax import lax
import numpy as np

D_MODEL = 1024
BATCH = 2
SEQ = 16384
DEPTH = 2

SSD_HEADS = 16
SSD_HEAD_DIM = 64
SSD_INNER = SSD_HEADS * SSD_HEAD_DIM
SSD_GROUPS = 4
SSD_STATE = 128
SSD_CONV = 4
SSD_CHUNK = 128
SSD_XBC = SSD_INNER + 2 * SSD_GROUPS * SSD_STATE

MOBA_HEADS = 8
MOBA_HEAD_DIM = 128
MOBA_INNER = MOBA_HEADS * MOBA_HEAD_DIM
MOBA_BLOCK = 256
MOBA_TOPK = 3
MOBA_QCHUNK = 32

REL_BUCKETS = 32
REL_MAX_DIST = 1024

RWKV_HEADS = 16
RWKV_HEAD_DIM = 64
RWKV_INNER = RWKV_HEADS * RWKV_HEAD_DIM
RWKV_DECAY_LORA = 64
RWKV_ICLR_LORA = 64
RWKV_GATE_LORA = 128
RWKV_IN = 3 * RWKV_INNER + RWKV_DECAY_LORA + RWKV_ICLR_LORA + RWKV_GATE_LORA
RWKV_LN_EPS = 64e-5

PEER_HEADS = 8
PEER_KEYS = 128
PEER_EXPERTS = PEER_KEYS * PEER_KEYS
PEER_TOPK = 16
PEER_KEY_DIM = 256
PEER_CHUNK = 128

N_BRANCH = 3
IN_SSD = SSD_INNER + SSD_XBC + SSD_HEADS
IN_MOBA = 3 * MOBA_INNER
IN_TOTAL = IN_SSD + IN_MOBA + RWKV_IN
NORM_EPS = 1e-6
NEG = -1e30

kernel_name = "hybrid_ssd_moba_rwkv7_peer"


def rms_norm(x, g, eps=NORM_EPS):
    xf = x.astype(jnp.float32)
    return xf * lax.rsqrt(jnp.mean(xf * xf, axis=-1, keepdims=True) + eps) * g.astype(jnp.float32)


def t5_bucket(dist):
    max_exact = REL_BUCKETS // 2
    d = jnp.maximum(dist, 0)
    df = jnp.maximum(d, 1).astype(jnp.float32)
    large = max_exact + (jnp.log(df / max_exact) / math.log(REL_MAX_DIST / max_exact)
                         * (REL_BUCKETS - max_exact)).astype(jnp.int32)
    large = jnp.minimum(large, REL_BUCKETS - 1)
    return jnp.where(d < max_exact, d, large)


def ssd_mixer(zxbcdt, conv_w, conv_b, dt_bias, a_log, d_skip, norm_g):
    f32 = jnp.float32
    bsz, slen, _ = zxbcdt.shape
    G, E, P, N, Lc = SSD_GROUPS, SSD_HEADS // SSD_GROUPS, SSD_HEAD_DIM, SSD_STATE, SSD_CHUNK
    nc = slen // Lc
    zxbcdt = zxbcdt.astype(f32)
    z = zxbcdt[..., :SSD_INNER]
    xbc = zxbcdt[..., SSD_INNER:SSD_INNER + SSD_XBC]
    dt = zxbcdt[..., SSD_INNER + SSD_XBC:]
    xbc = lax.conv_general_dilated(xbc, conv_w.astype(f32)[:, None, :], window_strides=(1,),
                                   padding=[(SSD_CONV - 1, 0)], dimension_numbers=("NWC", "WIO", "NWC"),
                                   feature_group_count=SSD_XBC)
    xbc = jax.nn.silu(xbc + conv_b.astype(f32))
    xs = xbc[..., :SSD_INNER]
    b_in = xbc[..., SSD_INNER:SSD_INNER + G * N].reshape(bsz, nc, Lc, G, N)
    c_in = xbc[..., SSD_INNER + G * N:].reshape(bsz, nc, Lc, G, N)
    dt = jax.nn.softplus(dt + dt_bias.astype(f32))
    a = -jnp.exp(a_log.astype(f32))
    xdt = xs.reshape(bsz, nc, Lc, G, E, P) * dt.reshape(bsz, nc, Lc, G, E)[..., None]
    acs = jnp.cumsum((dt * a).reshape(bsz, nc, Lc, G, E).transpose(0, 1, 3, 4, 2), axis=-1)
    causal = jnp.tril(jnp.ones((Lc, Lc), dtype=bool))
    seg = jnp.exp(jnp.where(causal, acs[..., :, None] - acs[..., None, :], -jnp.inf))
    cb = jnp.einsum("bclgn,bcsgn->bcgls", c_in, b_in)
    y_diag = jnp.einsum("bcgels,bcsgep->bclgep", cb[:, :, :, None] * seg, xdt)
    decay_to_end = jnp.exp(acs[..., -1:] - acs)
    chunk_states = jnp.einsum("bclgn,bcgel,bclgep->bcgepn", b_in, decay_to_end, xdt)
    chunk_decay = jnp.exp(acs[..., -1])

    def carry_state(h, inp):
        st, dec = inp
        return h * dec[..., None, None] + st, h

    h0 = jnp.zeros((bsz, G, E, P, N), f32)
    _, h_in = lax.scan(carry_state, h0, (jnp.moveaxis(chunk_states, 1, 0), jnp.moveaxis(chunk_decay, 1, 0)))
    h_in = jnp.moveaxis(h_in, 0, 1)
    y_off = jnp.einsum("bclgn,bcgepn,bcgel->bclgep", c_in, h_in, jnp.exp(acs))
    y = (y_diag + y_off).reshape(bsz, slen, SSD_INNER) + xs * jnp.repeat(d_skip.astype(f32), P)
    y = y * jax.nn.silu(z)
    yg = y.reshape(bsz, slen, G, SSD_INNER // G)
    yg = yg * lax.rsqrt(jnp.mean(yg * yg, axis=-1, keepdims=True) + NORM_EPS)
    return yg.reshape(bsz, slen, SSD_INNER) * norm_g.astype(f32)


def moba_mixer(qkv, q_norm_g, k_norm_g, rel_bias):
    f32 = jnp.float32
    bsz, slen, _ = qkv.shape
    H, Dh, BLK, QC = MOBA_HEADS, MOBA_HEAD_DIM, MOBA_BLOCK, MOBA_QCHUNK
    qkv = qkv.astype(f32)

    def heads(t):
        return t.reshape(bsz, slen, H, Dh).transpose(0, 2, 1, 3)

    q = rms_norm(heads(qkv[..., :MOBA_INNER]), q_norm_g)
    k = rms_norm(heads(qkv[..., MOBA_INNER:2 * MOBA_INNER]), k_norm_g)
    v = heads(qkv[..., 2 * MOBA_INNER:])
    nb = -(-slen // BLK)
    s_pad = nb * BLK
    padw = ((0, 0), (0, 0), (0, s_pad - slen), (0, 0))
    q, k, v = jnp.pad(q, padw), jnp.pad(k, padw), jnp.pad(v, padw)
    kb = k.reshape(bsz, H, nb, BLK, Dh)
    vb = v.reshape(bsz, H, nb, BLK, Dh)
    kmean = jnp.mean(kb, axis=3)
    n_sel = min(MOBA_TOPK, nb)
    nqc = s_pad // QC
    q_chunks = q.reshape(bsz, H, nqc, QC, Dh).transpose(2, 0, 1, 3, 4)
    bias_t = rel_bias.astype(f32).T
    scale = Dh ** -0.5
    b_idx = jnp.arange(bsz)[:, None, None, None]
    h_idx = jnp.arange(H)[None, :, None, None]

    def one_chunk(args):
        qi, c = args
        q_pos = c * QC + jnp.arange(QC, dtype=jnp.int32)
        qblk = (c * QC) // BLK
        gate = jnp.einsum("bhqd,bhnd->bhqn", qi, kmean)
        gate = jnp.where(jnp.arange(nb) < qblk, gate, NEG)
        _, sel = lax.top_k(gate, n_sel)
        sel_valid = jnp.arange(n_sel) < qblk
        k_sel = kb[b_idx, h_idx, sel]
        v_sel = vb[b_idx, h_idx, sel]
        s_past = jnp.einsum("bhqd,bhqjkd->bhqjk", qi, k_sel) * scale
        k_pos_past = sel[..., None] * BLK + jnp.arange(BLK, dtype=jnp.int32)
        bias_past = bias_t[h_idx[..., None], t5_bucket(q_pos[:, None, None] - k_pos_past)]
        s_past = jnp.where(sel_valid[:, None], s_past + bias_past, NEG)
        k_own = lax.dynamic_slice_in_dim(kb, qblk, 1, axis=2)[:, :, 0]
        v_own = lax.dynamic_slice_in_dim(vb, qblk, 1, axis=2)[:, :, 0]
        dist = q_pos[:, None] - (qblk * BLK + jnp.arange(BLK, dtype=jnp.int32))[None, :]
        s_own = jnp.einsum("bhqd,bhkd->bhqk", qi, k_own) * scale + bias_t[:, t5_bucket(dist)]
        s_own = jnp.where(dist >= 0, s_own, NEG)
        logits = jnp.concatenate([s_own, s_past.reshape(bsz, H, QC, n_sel * BLK)], axis=-1)
        p = jax.nn.softmax(logits, axis=-1)
        p_past = p[..., BLK:].reshape(bsz, H, QC, n_sel, BLK)
        return (jnp.einsum("bhqk,bhkd->bhqd", p[..., :BLK], v_own)
                + jnp.einsum("bhqjk,bhqjkd->bhqd", p_past, v_sel))

    out = lax.map(one_chunk, (q_chunks, jnp.arange(nqc, dtype=jnp.int32)))
    out = out.transpose(1, 2, 0, 3, 4).reshape(bsz, H, s_pad, Dh)[:, :, :slen]
    return out.transpose(0, 2, 1, 3).reshape(bsz, slen, MOBA_INNER)


def rwkv7_mixer(proj, mu, w0, w_w2, a0, w_a2, w_g2, k_k, k_a, r_k, lnx_g, lnx_b):
    f32 = jnp.float32
    bsz, slen, _ = proj.shape
    H, N, C = RWKV_HEADS, RWKV_HEAD_DIM, RWKV_INNER
    proj = proj.astype(f32)
    prev = jnp.pad(proj, ((0, 0), (1, 0), (0, 0)))[:, :-1]
    xs = proj + (prev - proj) * mu.astype(f32)
    o1, o2 = 3 * C + RWKV_DECAY_LORA, 3 * C + RWKV_DECAY_LORA + RWKV_ICLR_LORA
    r, k, v = xs[..., :C], xs[..., C:2 * C], xs[..., 2 * C:3 * C]
    wl, al, gl = xs[..., 3 * C:o1], xs[..., o1:o2], xs[..., o2:]
    w = -jax.nn.softplus(-(w0.astype(f32) + jnp.tanh(wl) @ w_w2.astype(f32))) - 0.5
    decay = jnp.exp(-jnp.exp(w))
    a = jax.nn.sigmoid(a0.astype(f32) + al @ w_a2.astype(f32))
    g = jax.nn.sigmoid(gl) @ w_g2.astype(f32)
    kk = (k * k_k.astype(f32)).reshape(bsz, slen, H, N)
    kk = kk * lax.rsqrt(jnp.maximum(jnp.sum(kk * kk, axis=-1, keepdims=True), 1e-12))
    kk = kk.reshape(bsz, slen, C)
    k = k * (1.0 + (a - 1.0) * k_a.astype(f32))

    def tm(t):
        return t.reshape(bsz, slen, H, N).transpose(1, 0, 2, 3)

    def step(state, inp):
        r_t, w_t, k_t, v_t, a_t, b_t = inp
        sa = jnp.einsum("bhvk,bhk->bhv", state, a_t)
        state = state * w_t[:, :, None, :] + sa[..., None] * b_t[:, :, None, :] + v_t[..., None] * k_t[:, :, None, :]
        return state, jnp.einsum("bhvk,bhk->bhv", state, r_t)

    s0 = jnp.zeros((bsz, H, N, N), f32)
    _, y = lax.scan(step, s0, (tm(r), tm(decay), tm(k), tm(v), tm(-kk), tm(kk * a)))
    y = y.transpose(1, 0, 2, 3)
    mean = jnp.mean(y, axis=-1, keepdims=True)
    var = jnp.mean(jnp.square(y - mean), axis=-1, keepdims=True)
    y = ((y - mean) * lax.rsqrt(var + RWKV_LN_EPS)).reshape(bsz, slen, C) * lnx_g.astype(f32) + lnx_b.astype(f32)
    rh, kh, vh = r.reshape(bsz, slen, H, N), k.reshape(bsz, slen, H, N), v.reshape(bsz, slen, H, N)
    bonus = jnp.sum(rh * kh * r_k.astype(f32), axis=-1, keepdims=True) * vh
    return (y + bonus.reshape(bsz, slen, C)) * g


def peer_ffn(h, wq, k1, k2, u, v):
    f32 = jnp.float32
    bsz, slen, dm = h.shape
    T, PH, half = PEER_CHUNK, PEER_HEADS, PEER_KEY_DIM // 2
    nc = slen // T
    h_chunks = h.astype(f32).reshape(bsz, nc, T, dm).transpose(1, 0, 2, 3)
    wq, k1, k2 = wq.astype(f32), k1.astype(f32), k2.astype(f32)

    def one_chunk(hi):
        q = (hi @ wq).reshape(bsz, T, PH, PEER_KEY_DIM)
        s1 = jnp.einsum("bthd,nd->bthn", q[..., :half], k1)
        s2 = jnp.einsum("bthd,nd->bthn", q[..., half:], k2)
        v1, i1 = lax.top_k(s1, PEER_TOPK)
        v2, i2 = lax.top_k(s2, PEER_TOPK)
        cand = (v1[..., :, None] + v2[..., None, :]).reshape(bsz, T, PH, PEER_TOPK * PEER_TOPK)
        cand_idx = (i1[..., :, None] * PEER_KEYS + i2[..., None, :]).reshape(bsz, T, PH, PEER_TOPK * PEER_TOPK)
        sc, pos = lax.top_k(cand, PEER_TOPK)
        idx = jnp.take_along_axis(cand_idx, pos, axis=-1)
        gate = jax.nn.softmax(sc, axis=-1)
        act = jax.nn.gelu(jnp.einsum("btd,bthkd->bthk", hi, u[idx].astype(f32)), approximate=False)
        return jnp.einsum("bthk,bthkd->btd", gate * act, v[idx].astype(f32))

    out = lax.map(one_chunk, h_chunks)
    return out.transpose(1, 0, 2, 3).reshape(bsz, slen, dm)


def setup_inputs(seed: int = 0) -> dict:
    key = jax.random.key(seed)
    ks = iter(jax.random.split(key, 48))
    L, D = DEPTH, D_MODEL
    f32 = jnp.float32

    def nrm(shape, scale):
        return jax.random.normal(next(ks), shape, f32) * scale

    def unif(shape, lo, hi):
        return jax.random.uniform(next(ks), shape, f32, lo, hi)

    def gain(shape):
        return 1.0 + nrm(shape, 0.05)

    dt0 = jnp.exp(unif((L, SSD_HEADS), math.log(1e-3), math.log(1e-1)))
    inp = {}
    inp["x"] = nrm((BATCH, SEQ, D), 1.0)
    inp["rel_bias"] = nrm((REL_BUCKETS, MOBA_HEADS), 0.5)
    inp["norm1_g"] = gain((L, D))
    inp["w_in"] = nrm((L, D, IN_TOTAL), D ** -0.5)
    inp["conv_w"] = nrm((L, SSD_CONV, SSD_XBC), 0.5)
    inp["conv_b"] = nrm((L, SSD_XBC), 0.02)
    inp["dt_bias"] = dt0 + jnp.log(-jnp.expm1(-dt0))
    inp["a_log"] = jnp.log(unif((L, SSD_HEADS), 1.0, 16.0))
    inp["d_skip"] = 1.0 + nrm((L, SSD_HEADS), 0.1)
    inp["ssd_norm_g"] = gain((L, SSD_INNER))
    inp["q_norm_g"] = gain((L, MOBA_HEAD_DIM))
    inp["k_norm_g"] = gain((L, MOBA_HEAD_DIM))
    inp["rwkv_mu"] = unif((L, RWKV_IN), 0.0, 1.0)
    inp["w0"] = unif((L, RWKV_INNER), -7.0, -1.0)
    inp["w_w2"] = nrm((L, RWKV_DECAY_LORA, RWKV_INNER), 0.1)
    inp["a0"] = nrm((L, RWKV_INNER), 0.5)
    inp["w_a2"] = nrm((L, RWKV_ICLR_LORA, RWKV_INNER), RWKV_ICLR_LORA ** -0.5)
    inp["w_g2"] = nrm((L, RWKV_GATE_LORA, RWKV_INNER), RWKV_GATE_LORA ** -0.5)
    inp["k_k"] = 0.85 + nrm((L, RWKV_INNER), 0.05)
    inp["k_a"] = 1.0 + nrm((L, RWKV_INNER), 0.05)
    inp["r_k"] = nrm((L, RWKV_HEADS, RWKV_HEAD_DIM), 0.1)
    inp["lnx_g"] = gain((L, RWKV_INNER))
    inp["lnx_b"] = nrm((L, RWKV_INNER), 0.02)
    inp["p_ssd"] = nrm((L, SSD_INNER, D), SSD_INNER ** -0.5)
    inp["p_moba"] = nrm((L, MOBA_INNER, D), MOBA_INNER ** -0.5)
    inp["p_rwkv"] = nrm((L, RWKV_INNER, D), RWKV_INNER ** -0.5)
    inp["w_gate"] = nrm((L, D, N_BRANCH * D), D ** -0.5)
    inp["b_gate"] = nrm((L, N_BRANCH * D), 0.02)
    inp["w_out"] = nrm((L, D, D), 0.5 * D ** -0.5)
    inp["norm2_g"] = gain((L, D))
    inp["peer_wq"] = nrm((L, D, PEER_HEADS * PEER_KEY_DIM), D ** -0.5)
    inp["peer_k1"] = nrm((L, PEER_KEYS, PEER_KEY_DIM // 2), (PEER_KEY_DIM // 2) ** -0.5)
    inp["peer_k2"] = nrm((L, PEER_KEYS, PEER_KEY_DIM // 2), (PEER_KEY_DIM // 2) ** -0.5)
    inp["peer_u"] = nrm((L, PEER_EXPERTS, D), D ** -0.5)
    inp["peer_v"] = nrm((L, PEER_EXPERTS, D), 0.1)
    return inp


def reference(x, rel_bias, norm1_g, w_in, conv_w, conv_b, dt_bias, a_log, d_skip, ssd_norm_g,
              q_norm_g, k_norm_g, rwkv_mu, w0, w_w2, a0, w_a2, w_g2, k_k, k_a, r_k, lnx_g, lnx_b,
              p_ssd, p_moba, p_rwkv, w_gate, b_gate, w_out, norm2_g,
              peer_wq, peer_k1, peer_k2, peer_u, peer_v):
    bsz, slen, dm = x.shape
    for l in range(DEPTH):
        h = rms_norm(x, norm1_g[l]).astype(x.dtype)
        proj = h @ w_in[l]
        u_ssd = proj[..., :IN_SSD]
        u_moba = proj[..., IN_SSD:IN_SSD + IN_MOBA]
        u_rwkv = proj[..., IN_SSD + IN_MOBA:]
        y_a = ssd_mixer(u_ssd, conv_w[l], conv_b[l], dt_bias[l], a_log[l], d_skip[l], ssd_norm_g[l])
        y_b = moba_mixer(u_moba, q_norm_g[l], k_norm_g[l], rel_bias)
        y_c = rwkv7_mixer(u_rwkv, rwkv_mu[l], w0[l], w_w2[l], a0[l], w_a2[l], w_g2[l],
                          k_k[l], k_a[l], r_k[l], lnx_g[l], lnx_b[l])
        gates = jax.nn.sigmoid((h @ w_gate[l] + b_gate[l]).astype(jnp.float32)).reshape(bsz, slen, N_BRANCH, dm)
        merged = (gates[..., 0, :] * (y_a @ p_ssd[l].astype(jnp.float32))
                  + gates[..., 1, :] * (y_b @ p_moba[l].astype(jnp.float32))
                  + gates[..., 2, :] * (y_c @ p_rwkv[l].astype(jnp.float32)))
        x = x + (merged.astype(x.dtype) @ w_out[l])
        h2 = rms_norm(x, norm2_g[l]).astype(x.dtype)
        x = x + peer_ffn(h2, peer_wq[l], peer_k1[l], peer_k2[l], peer_u[l], peer_v[l]).astype(x.dtype)
    return x
```

```python
import functools
import math

import numpy as np
import jax
import jax.numpy as jnp
from jax import lax
from jax.experimental import pallas as pl
from jax.experimental.pallas import tpu as pltpu

F32 = jnp.float32
BF16 = jnp.bfloat16
HIGHEST = lax.Precision.HIGHEST

D_MODEL = 1024
NORM_EPS = 1e-6
NEG = -1e30
VMEM_LIMIT = 56 * 1024 * 1024

SSD_HEADS = 16
SSD_HEAD_DIM = 64
SSD_INNER = 1024
SSD_GROUPS = 4
SSD_STATE = 128
SSD_CONV = 4
SSD_CHUNK = 128
SSD_XBC = SSD_INNER + 2 * SSD_GROUPS * SSD_STATE
DT_PAD = 128

MOBA_HEADS = 8
MOBA_HEAD_DIM = 128
MOBA_INNER = 1024
MOBA_BLOCK = 256
MOBA_TOPK = 3
REL_BUCKETS = 32
REL_MAX_DIST = 1024
MOBA_NEAR = 5

RWKV_HEADS = 16
RWKV_HEAD_DIM = 64
RWKV_INNER = 1024
RWKV_DECAY_LORA = 64
RWKV_ICLR_LORA = 64
RWKV_GATE_LORA = 128
RWKV_IN = 3 * RWKV_INNER + RWKV_DECAY_LORA + RWKV_ICLR_LORA + RWKV_GATE_LORA
RWKV_LN_EPS = 64e-5
RWKV_CHUNK = 64
RWKV_GROUP = 4

PEER_HEADS = 8
PEER_KEYS = 128
PEER_TOPK = 16
PEER_KEY_DIM = 256

IN_SSD = SSD_INNER + SSD_XBC + SSD_HEADS
IN_MOBA = 3 * MOBA_INNER


def _dot(a, b, precision=None):
    return jnp.dot(a, b, preferred_element_type=F32, precision=precision)


def _dot_nt(a, b, precision=None):
    return lax.dot_general(a, b, (((1,), (1,)), ((), ())), preferred_element_type=F32, precision=precision)


def _dot_tn(a, b, precision=None):
    return lax.dot_general(a, b, (((0,), (0,)), ((), ())), preferred_element_type=F32, precision=precision)


def _rms(x, g):
    return x * lax.rsqrt(jnp.mean(x * x, axis=-1, keepdims=True) + NORM_EPS) * g


def _silu(x):
    return x * jax.nn.sigmoid(x)


def _params(*sem):
    return pltpu.CompilerParams(dimension_semantics=sem, vmem_limit_bytes=VMEM_LIMIT)


def _norm_matmul_kernel(x_ref, g_ref, w_ref, o_ref, *, col_chunk):
    hb = _rms(x_ref[...], g_ref[...]).astype(BF16)
    for c in range(0, o_ref.shape[1], col_chunk):
        o_ref[:, c:c + col_chunk] = _dot(hb, w_ref[:, c:c + col_chunk])


def norm_matmul(x, g, w, tm=512, col_chunk=512):
    t, d = x.shape
    n = w.shape[1]
    col_chunk = math.gcd(n, col_chunk)
    return pl.pallas_call(
        functools.partial(_norm_matmul_kernel, col_chunk=col_chunk),
        grid=(t // tm,),
        in_specs=[pl.BlockSpec((tm, d), lambda i: (i, 0)),
                  pl.BlockSpec((1, d), lambda i: (0, 0)),
                  pl.BlockSpec((d, n), lambda i: (0, 0))],
        out_specs=pl.BlockSpec((tm, n), lambda i: (i, 0)),
        out_shape=jax.ShapeDtypeStruct((t, n), F32),
        compiler_params=_params("parallel"),
        name="norm_matmul",
    )(x, g.reshape(1, d), w)


def _merge_kernel(x_ref, ya_ref, yb_ref, yc_ref, g_ref, wg_ref, bg_ref, pa_ref, pb_ref, pc_ref, wo_ref, o_ref):
    x = x_ref[...]
    hb = _rms(x, g_ref[...]).astype(BF16)
    d = x.shape[1]
    merged = None
    for i, (y_ref, p_ref) in enumerate(((ya_ref, pa_ref), (yb_ref, pb_ref), (yc_ref, pc_ref))):
        gate = jax.nn.sigmoid(_dot(hb, wg_ref[:, i * d:(i + 1) * d]) + bg_ref[:, i * d:(i + 1) * d])
        term = gate * _dot(y_ref[...], p_ref[...])
        merged = term if merged is None else merged + term
    o_ref[...] = x + _dot(merged.astype(BF16), wo_ref[...])


def merge(x, ya, yb, yc, g, w_gate, b_gate, pa, pb, pc, w_out, tm=256):
    t, d = x.shape
    row = lambda i: (i, 0)
    fixed = lambda i: (0, 0)
    return pl.pallas_call(
        _merge_kernel,
        grid=(t // tm,),
        in_specs=[pl.BlockSpec((tm, d), row), pl.BlockSpec((tm, d), row), pl.BlockSpec((tm, d), row),
                  pl.BlockSpec((tm, d), row), pl.BlockSpec((1, d), fixed),
                  pl.BlockSpec((d, 3 * d), fixed), pl.BlockSpec((1, 3 * d), fixed),
                  pl.BlockSpec((d, d), fixed), pl.BlockSpec((d, d), fixed), pl.BlockSpec((d, d), fixed),
                  pl.BlockSpec((d, d), fixed)],
        out_specs=pl.BlockSpec((tm, d), row),
        out_shape=jax.ShapeDtypeStruct((t, d), F32),
        compiler_params=_params("parallel"),
        name="merge",
    )(x, ya, yb, yc, g.reshape(1, d), w_gate, b_gate.reshape(1, 3 * d), pa, pb, pc, w_out)


def _ssd_kernel(xbc_ref, z_ref, dt_ref, cw_ref, cb_ref, dtb_ref, a_ref, dsk_ref, ng_ref, rexp_ref, tri_ref,
                o_ref, buf_ref, state_ref):
    lc = SSD_CHUNK
    c = pl.program_id(1)

    @pl.when(c == 0)
    def _():
        buf_ref[0:8, :] = jnp.zeros((8, SSD_XBC), F32)
        state_ref[...] = jnp.zeros_like(state_ref)

    cur = xbc_ref[...]
    buf_ref[8:8 + lc, :] = cur
    acc = cb_ref[...] + cw_ref[0:1, :] * buf_ref[5:5 + lc, :]
    for k in range(1, SSD_CONV):
        acc = acc + cw_ref[k:k + 1, :] * buf_ref[5 + k:5 + k + lc, :]
    buf_ref[0:8, :] = cur[lc - 8:lc, :]
    xbc = _silu(acc)
    xs = xbc[:, :SSD_INNER]

    dt = jax.nn.softplus(dt_ref[...] + dtb_ref[...])
    acs = _dot(tri_ref[...], dt * a_ref[...], HIGHEST)
    acs_t = acs.T
    rexp = rexp_ref[...]
    dt_x = _dot(dt, rexp, HIGHEST)
    e_x = _dot(jnp.exp(acs), rexp, HIGHEST)
    dte_x = _dot(jnp.exp(acs[lc - 1:lc, :] - acs), rexp, HIGHEST)
    xdt = xs * dt_x
    xdt_b = xdt.astype(BF16)
    xw_b = (xdt * dte_x).astype(BF16)
    chunk_decay = e_x[lc - 1:lc, :]

    row = lax.broadcasted_iota(jnp.int32, (lc, lc), 0)
    col = lax.broadcasted_iota(jnp.int32, (lc, lc), 1)
    causal = row >= col
    gw = SSD_INNER // SSD_GROUPS
    hpg = SSD_HEADS // SSD_GROUPS
    n = SSD_STATE
    ys = []
    for g in range(SSD_GROUPS):
        b_g = xbc[:, SSD_INNER + g * n:SSD_INNER + (g + 1) * n].astype(BF16)
        c_g = xbc[:, SSD_INNER + (SSD_GROUPS + g) * n:SSD_INNER + (SSD_GROUPS + g + 1) * n].astype(BF16)
        cb = _dot_nt(c_g, b_g)
        s_in = state_ref[:, g * gw:(g + 1) * gw]
        y_off = _dot(c_g, s_in.astype(BF16)) * e_x[:, g * gw:(g + 1) * gw]
        state_ref[:, g * gw:(g + 1) * gw] = (chunk_decay[:, g * gw:(g + 1) * gw] * s_in
                                             + _dot_tn(b_g, xw_b[:, g * gw:(g + 1) * gw]))
        diag = []
        for e in range(hpg):
            h = g * hpg + e
            seg = jnp.exp(jnp.where(causal, acs[:, h:h + 1] - acs_t[h:h + 1, :], NEG))
            m = (cb * seg).astype(BF16)
            diag.append(_dot(m, xdt_b[:, h * SSD_HEAD_DIM:(h + 1) * SSD_HEAD_DIM]))
        ys.append(jnp.concatenate(diag, axis=-1) + y_off)
    y = jnp.concatenate(ys, axis=-1) + xs * dsk_ref[...]
    y = y * _silu(z_ref[...])
    for g in range(SSD_GROUPS):
        yg = y[:, g * gw:(g + 1) * gw]
        yg = yg * lax.rsqrt(jnp.mean(yg * yg, axis=-1, keepdims=True) + NORM_EPS)
        o_ref[:, g * gw:(g + 1) * gw] = (yg * ng_ref[:, g * gw:(g + 1) * gw]).astype(o_ref.dtype)


def ssd(proj, bsz, slen, conv_w, conv_b, dt_bias, a_log, d_skip, norm_g):
    lc = SSD_CHUNK
    nc = slen // lc
    t = bsz * slen
    pad = DT_PAD - SSD_HEADS
    dtb = jnp.pad(dt_bias.astype(F32), (0, pad)).reshape(1, DT_PAD)
    a = jnp.pad(-jnp.exp(a_log.astype(F32)), (0, pad)).reshape(1, DT_PAD)
    dsk = jnp.repeat(d_skip.astype(F32), SSD_HEAD_DIM).reshape(1, SSD_INNER)
    rexp = jnp.asarray((np.arange(DT_PAD)[:, None] == np.arange(SSD_INNER)[None, :] // SSD_HEAD_DIM)
                       .astype(np.float32))
    tri = jnp.asarray(np.tril(np.ones((lc, lc), np.float32)))
    rowmap = lambda b, c: (b * nc + c, 0)
    fixed = lambda b, c: (0, 0)
    zblk = SSD_XBC // SSD_INNER
    dtblk = (SSD_XBC + SSD_INNER) // DT_PAD
    return pl.pallas_call(
        _ssd_kernel,
        grid=(bsz, nc),
        in_specs=[pl.BlockSpec((lc, SSD_XBC), rowmap),
                  pl.BlockSpec((lc, SSD_INNER), lambda b, c: (b * nc + c, zblk)),
                  pl.BlockSpec((lc, DT_PAD), lambda b, c: (b * nc + c, dtblk)),
                  pl.BlockSpec((SSD_CONV, SSD_XBC), fixed), pl.BlockSpec((1, SSD_XBC), fixed),
                  pl.BlockSpec((1, DT_PAD), fixed), pl.BlockSpec((1, DT_PAD), fixed),
                  pl.BlockSpec((1, SSD_INNER), fixed), pl.BlockSpec((1, SSD_INNER), fixed),
                  pl.BlockSpec((DT_PAD, SSD_INNER), fixed), pl.BlockSpec((lc, lc), fixed)],
        out_specs=pl.BlockSpec((lc, SSD_INNER), rowmap),
        out_shape=jax.ShapeDtypeStruct((t, SSD_INNER), BF16),
        scratch_shapes=[pltpu.VMEM((lc + 8, SSD_XBC), F32), pltpu.VMEM((SSD_STATE, SSD_INNER), F32)],
        compiler_params=_params("parallel", "arbitrary"),
        name="ssd",
    )(proj, proj, proj, conv_w.astype(F32), conv_b.astype(F32).reshape(1, SSD_XBC), dtb, a, dsk,
      norm_g.astype(F32).reshape(1, SSD_INNER), rexp, tri)


def _ssd_weight(w_in_l):
    w = w_in_l[:, :IN_SSD]
    z, xbc, dt = w[:, :SSD_INNER], w[:, SSD_INNER:SSD_INNER + SSD_XBC], w[:, SSD_INNER + SSD_XBC:]
    dt = jnp.pad(dt, ((0, 0), (0, DT_PAD - SSD_HEADS)))
    return jnp.concatenate([xbc, z, dt], axis=1).astype(BF16)


def _t5_bucket(dist):
    max_exact = REL_BUCKETS // 2
    d = jnp.maximum(dist, 0)
    df = jnp.maximum(d, 1).astype(F32)
    large = max_exact + (jnp.log(df / max_exact) / math.log(REL_MAX_DIST / max_exact)
                         * (REL_BUCKETS - max_exact)).astype(jnp.int32)
    large = jnp.minimum(large, REL_BUCKETS - 1)
    return jnp.where(d < max_exact, d, large)


def _bias_table_kernel(bias_ref, bucket_ref, o_ref):
    h = pl.program_id(0)
    for d in range(MOBA_NEAR):
        bk = bucket_ref[d]
        acc = jnp.where(bk < 0, NEG, 0.0)
        for b in range(REL_BUCKETS):
            acc = jnp.where(bk == b, bias_ref[h, b], acc)
        o_ref[d] = acc


def moba_bias_table(rel_bias):
    blk = MOBA_BLOCK
    q = jnp.arange(blk, dtype=jnp.int32)[None, :, None]
    k = jnp.arange(blk, dtype=jnp.int32)[None, None, :]
    dist = jnp.arange(MOBA_NEAR, dtype=jnp.int32)[:, None, None] * blk + q - k
    bucket = jnp.where(dist < 0, -1, _t5_bucket(dist))
    return pl.pallas_call(
        _bias_table_kernel,
        grid=(MOBA_HEADS,),
        in_specs=[pl.BlockSpec(memory_space=pltpu.SMEM),
                  pl.BlockSpec((MOBA_NEAR, blk, blk), lambda h: (0, 0, 0))],
        out_specs=pl.BlockSpec((None, MOBA_NEAR, blk, blk), lambda h: (h, 0, 0, 0)),
        out_shape=jax.ShapeDtypeStruct((MOBA_HEADS, MOBA_NEAR, blk, blk), F32),
        compiler_params=_params("arbitrary"),
        name="moba_bias_table",
    )(rel_bias.astype(F32).T, bucket)


def _moba_prep_kernel(q_ref, k_ref, v_ref, qg_ref, kg_ref, far_ref, qa_ref, ka_ref, vb_ref, kmean_ref):
    i = pl.program_id(2)
    blk, dh = q_ref.shape
    half = dh // 2

    @pl.when(i == 0)
    def _():
        kmean_ref[...] = jnp.zeros_like(kmean_ref)

    qn = _rms(q_ref[...], qg_ref[...])
    kn = _rms(k_ref[...], kg_ref[...])
    gate = _dot_nt(qn, kmean_ref[...], HIGHEST)
    col = lax.broadcasted_iota(jnp.int32, (blk, dh), 1)
    colf = col.astype(F32)
    valid = col < i
    g = jnp.where(valid, gate, NEG)
    sel = col < 0
    for _ in range(MOBA_TOPK):
        m = jnp.max(g, axis=-1, keepdims=True)
        idx = jnp.min(jnp.where(g == m, colf, float(dh)), axis=-1, keepdims=True)
        hit = colf == idx
        sel = sel | hit
        g = jnp.where(hit, -jnp.inf, g)
    sel = sel & valid

    far = far_ref[...]
    far_hi = far.astype(BF16).astype(F32)
    far_lo = far - far_hi
    sel_f = jnp.where(sel, 1.0, 0.0)
    low = col >= half
    sel2 = jnp.where(low, pltpu.roll(sel_f, half, 1), sel_f) > 0.5
    jcol = jnp.where(low, col - half, col)
    is_far = (i - jcol) >= MOBA_NEAR
    pen_hi = jnp.where(jcol == i, 0.0, jnp.where(sel2, jnp.where(is_far, far_hi, 0.0), NEG))
    pen_lo = jnp.where(sel2 & is_far, far_lo, 0.0)
    qa_ref[:, 0:dh] = (qn * (dh ** -0.5)).astype(BF16)
    qa_ref[:, dh:2 * dh] = jnp.where(low, pen_lo, pen_hi).astype(BF16)
    ka_ref[:, 0:dh] = kn.astype(BF16)
    ka_ref[:, dh:2 * dh] = jnp.where(jcol == i, 1.0, 0.0).astype(BF16)
    vb_ref[...] = v_ref[...].astype(BF16)
    kmean_ref[pl.ds(i, 1), :] = jnp.mean(kn, axis=0, keepdims=True)


def _moba_attn_kernel(qa_ref, ka_ref, v_ref, tab_ref, o_ref, m_ref, l_ref, acc_ref):
    i = pl.program_id(2)
    blk = qa_ref.shape[0]
    q = qa_ref[...]
    m_ref[...] = jnp.full_like(m_ref, 0.1 * NEG)
    l_ref[...] = jnp.zeros_like(l_ref)
    acc_ref[...] = jnp.zeros_like(acc_ref)

    def step(j, near):
        off = pl.multiple_of(j * blk, blk)
        s = _dot_nt(q, ka_ref[pl.ds(off, blk), :])
        if near:
            s = s + tab_ref[i - j]
        m_old = m_ref[...]
        m_new = jnp.maximum(m_old, jnp.max(s, axis=-1, keepdims=True))
        alpha = jnp.exp(m_old - m_new)
        p = jnp.exp(s - m_new)
        l_ref[...] = alpha * l_ref[...] + jnp.sum(p, axis=-1, keepdims=True)
        acc_ref[...] = alpha * acc_ref[...] + _dot(p.astype(BF16), v_ref[pl.ds(off, blk), :])
        m_ref[...] = m_new

    n_far = jnp.maximum(i - (MOBA_NEAR - 1), 0)

    def far_body(j, c):
        step(j, False)
        return c

    def near_body(j, c):
        step(j, True)
        return c

    lax.fori_loop(0, n_far, far_body, 0)
    lax.fori_loop(n_far, i + 1, near_body, 0)
    o_ref[...] = (acc_ref[...] / l_ref[...]).astype(o_ref.dtype)


def moba(proj, bsz, slen, q_norm_g, k_norm_g, bias_table, far_bias):
    blk, dh, nh = MOBA_BLOCK, MOBA_HEAD_DIM, MOBA_HEADS
    nb = slen // blk
    assert slen % blk == 0 and nb <= dh // 2
    t = bsz * slen
    far = jnp.broadcast_to(far_bias.astype(F32)[:, None, None], (nh, 1, dh))
    qa, ka, vb = pl.pallas_call(
        _moba_prep_kernel,
        grid=(bsz, nh, nb),
        in_specs=[pl.BlockSpec((blk, dh), lambda b, h, i: (b * nb + i, h)),
                  pl.BlockSpec((blk, dh), lambda b, h, i: (b * nb + i, nh + h)),
                  pl.BlockSpec((blk, dh), lambda b, h, i: (b * nb + i, 2 * nh + h)),
                  pl.BlockSpec((1, dh), lambda b, h, i: (0, 0)),
                  pl.BlockSpec((1, dh), lambda b, h, i: (0, 0)),
                  pl.BlockSpec((None, 1, dh), lambda b, h, i: (h, 0, 0))],
        out_specs=[pl.BlockSpec((None, None, blk, 2 * dh), lambda b, h, i: (b, h, i, 0)),
                   pl.BlockSpec((None, None, blk, 2 * dh), lambda b, h, i: (b, h, i, 0)),
                   pl.BlockSpec((None, None, blk, dh), lambda b, h, i: (b, h, i, 0))],
        out_shape=[jax.ShapeDtypeStruct((bsz, nh, slen, 2 * dh), BF16),
                   jax.ShapeDtypeStruct((bsz, nh, slen, 2 * dh), BF16),
                   jax.ShapeDtypeStruct((bsz, nh, slen, dh), BF16)],
        scratch_shapes=[pltpu.VMEM((dh, dh), F32)],
        compiler_params=_params("parallel", "parallel", "arbitrary"),
        name="moba_prep",
    )(proj, proj, proj, q_norm_g.astype(F32).reshape(1, dh), k_norm_g.astype(F32).reshape(1, dh), far)
    return pl.pallas_call(
        _moba_attn_kernel,
        grid=(bsz, nh, nb),
        in_specs=[pl.BlockSpec((None, None, blk, 2 * dh), lambda b, h, i: (b, h, i, 0)),
                  pl.BlockSpec((None, None, slen, 2 * dh), lambda b, h, i: (b, h, 0, 0)),
                  pl.BlockSpec((None, None, slen, dh), lambda b, h, i: (b, h, 0, 0)),
                  pl.BlockSpec((None, MOBA_NEAR, blk, blk), lambda b, h, i: (h, 0, 0, 0))],
        out_specs=pl.BlockSpec((blk, dh), lambda b, h, i: (b * nb + i, h)),
        out_shape=jax.ShapeDtypeStruct((t, nh * dh), BF16),
        scratch_shapes=[pltpu.VMEM((blk, 1), F32), pltpu.VMEM((blk, 1), F32), pltpu.VMEM((blk, dh), F32)],
        compiler_params=_params("parallel", "parallel", "arbitrary"),
        name="moba_attn",
    )(qa, ka, vb, bias_table)


def _rwkv_prep_kernel(cur_ref, prev_ref, mu_ref, w0_ref, ww2_ref, a0_ref, wa2_ref, wg2_ref, kk_ref, ka_ref,
                      hsum_ref, hexp_ref, r_ref, lw_ref, k_ref, v_ref, a_ref, b_ref, g_ref, buf_ref,
                      *, tiles_per_seq):
    i = pl.program_id(0)
    tm = cur_ref.shape[0]
    c = RWKV_INNER
    cur = cur_ref[...]
    first = (i % tiles_per_seq) == 0
    buf_ref[8:8 + tm, :] = cur
    buf_ref[0:8, :] = jnp.where(first, 0.0, prev_ref[...])
    prev = buf_ref[7:7 + tm, :]
    xs = cur + (prev - cur) * mu_ref[...]
    r, k, v = xs[:, :c], xs[:, c:2 * c], xs[:, 2 * c:3 * c]
    o1 = 3 * c + RWKV_DECAY_LORA
    o2 = o1 + RWKV_ICLR_LORA
    wl, al, gl = xs[:, 3 * c:o1], xs[:, o1:o2], xs[:, o2:]
    w = -jax.nn.softplus(-(w0_ref[...] + _dot(jnp.tanh(wl), ww2_ref[...], HIGHEST))) - 0.5
    a = jax.nn.sigmoid(a0_ref[...] + _dot(al, wa2_ref[...], HIGHEST))
    kk = k * kk_ref[...]
    ss = _dot(_dot(kk * kk, hsum_ref[...], HIGHEST), hexp_ref[...], HIGHEST)
    kk = kk * lax.rsqrt(jnp.maximum(ss, 1e-12))
    r_ref[...] = r
    lw_ref[...] = -jnp.exp(w)
    k_ref[...] = k * (1.0 + (a - 1.0) * ka_ref[...])
    v_ref[...] = v
    a_ref[...] = -kk
    b_ref[...] = kk * a
    g_ref[...] = _dot(jax.nn.sigmoid(gl), wg2_ref[...], HIGHEST)


def _bd_rows(y, mask):
    return jnp.where(mask, jnp.concatenate([y] * RWKV_GROUP, axis=0), 0.0)


def _rwkv_scan_kernel(r_ref, lw_ref, k_ref, v_ref, a_ref, b_ref, g_ref, rk_ref, lng_ref, lnb_ref,
                      tri_ref, lows_ref, lowi_ref, eye_ref, bd_ref, avg_ref, o_ref, s_ref, *, mxu_dtype):
    ci = pl.program_id(2)
    ch = RWKV_CHUNK

    @pl.when(ci == 0)
    def _():
        s_ref[...] = jnp.zeros_like(s_ref)

    prec = HIGHEST if mxu_dtype == F32 else None

    def mm(x, y):
        return _dot(x.astype(mxu_dtype), y.astype(mxu_dtype), prec)

    def mm_nt(x, y):
        return _dot_nt(x.astype(mxu_dtype), y.astype(mxu_dtype), prec)

    def mm_tn(x, y):
        return _dot_tn(x.astype(mxu_dtype), y.astype(mxu_dtype), prec)

    r, lw, k, v, a, b = r_ref[...], lw_ref[...], k_ref[...], v_ref[...], a_ref[...], b_ref[...]
    bd = bd_ref[...] > 0.5
    strict = lows_ref[...] > 0.5
    incl = lowi_ref[...] > 0.5

    cl = _dot(tri_ref[...], lw, HIGHEST)
    cl_end = cl[ch - 1:ch, :]
    e_neg = jnp.exp(-cl)
    a_t = a * jnp.exp(cl - lw)
    r_t = r * jnp.exp(cl)
    b_t = b * e_neg
    k_t = k * e_neg
    e_end = jnp.exp(cl_end - cl)
    b_h = b * e_end
    k_h = k * e_end

    xb = _bd_rows(b_t, bd)
    xk = _bd_rows(k_t, bd)
    a_ab = jnp.where(strict, mm_nt(a_t, xb), 0.0)
    a_ak = jnp.where(strict, mm_nt(a_t, xk), 0.0)
    q_rb = jnp.where(incl, mm_nt(r_t, xb), 0.0)
    q_rk = jnp.where(incl, mm_nt(r_t, xk), 0.0)

    t_inv = eye_ref[...] + a_ab
    p = a_ab
    for _ in range(int(math.log2(ch)) - 1):
        p = mm(p, _bd_rows(p, bd))
        t_inv = t_inv + mm(t_inv, _bd_rows(p, bd))

    vbd = _bd_rows(v, bd)
    w_t = mm(t_inv, _bd_rows(a_t, bd))
    u0 = mm(t_inv, _bd_rows(mm(a_ak, vbd), bd))
    z = r_t + mm(q_rb, _bd_rows(w_t, bd))
    y_loc = mm(q_rb, _bd_rows(u0, bd)) + mm(q_rk, vbd)
    s0 = s_ref[...]
    y = mm(z, s0) + y_loc
    eye_big = lax.broadcasted_iota(jnp.int32, s0.shape, 0) == lax.broadcasted_iota(jnp.int32, s0.shape, 1)
    m_s = jnp.where(bd, mm_tn(b_h, w_t), 0.0) + jnp.where(eye_big, jnp.exp(cl_end), 0.0)
    n_s = jnp.where(bd, mm_tn(b_h, u0) + mm_tn(k_h, v), 0.0)
    s_ref[...] = mm(m_s, s0) + n_s

    avg = avg_ref[...]
    mean = _dot(y, avg, HIGHEST)
    dlt = y - mean
    var = _dot(dlt * dlt, avg, HIGHEST)
    yn = dlt * lax.rsqrt(var + RWKV_LN_EPS) * lng_ref[...] + lnb_ref[...]
    bonus = _dot(r * k * rk_ref[...], avg, HIGHEST) * float(RWKV_HEAD_DIM) * v
    o_ref[...] = ((yn + bonus) * g_ref[...]).astype(o_ref.dtype)


def rwkv(proj, bsz, slen, mu, w0, w_w2, a0, w_a2, w_g2, k_k, k_a, r_k, lnx_g, lnx_b, tm=256, mxu_dtype=BF16):
    c, ch, grp, n = RWKV_INNER, RWKV_CHUNK, RWKV_GROUP, RWKV_HEAD_DIM
    t = bsz * slen
    f = lambda p: p.astype(F32).reshape(1, -1)
    heads = np.arange(c) // n
    hsum = jnp.asarray((heads[:, None] == np.arange(128)[None, :]).astype(np.float32))
    hexp = hsum.T
    row = lambda i: (i, 0)
    fixed = lambda i: (0, 0)
    wide = pl.BlockSpec((tm, c), row)
    vec = pl.BlockSpec((1, c), fixed)
    r, lw, k, v, a, b, g = pl.pallas_call(
        functools.partial(_rwkv_prep_kernel, tiles_per_seq=slen // tm),
        grid=(t // tm,),
        in_specs=[pl.BlockSpec((tm, RWKV_IN), row),
                  pl.BlockSpec((8, RWKV_IN), lambda i: (jnp.maximum(i * (tm // 8) - 1, 0), 0)),
                  pl.BlockSpec((1, RWKV_IN), fixed), vec,
                  pl.BlockSpec((RWKV_DECAY_LORA, c), fixed), vec,
                  pl.BlockSpec((RWKV_ICLR_LORA, c), fixed), pl.BlockSpec((RWKV_GATE_LORA, c), fixed),
                  vec, vec, pl.BlockSpec((c, 128), fixed), pl.BlockSpec((128, c), fixed)],
        out_specs=[wide] * 7,
        out_shape=[jax.ShapeDtypeStruct((t, c), F32)] * 7,
        scratch_shapes=[pltpu.VMEM((tm + 8, RWKV_IN), F32)],
        compiler_params=_params("parallel"),
        name="rwkv_prep",
    )(proj, proj, f(mu), f(w0), w_w2.astype(F32), f(a0), w_a2.astype(F32), w_g2.astype(F32), f(k_k), f(k_a),
      hsum, hexp)

    gw = grp * n
    nc = slen // ch
    tt = np.arange(ch)
    colt = np.arange(gw) % ch
    hrow = np.arange(gw) // n
    tri = jnp.asarray((tt[:, None] >= tt[None, :]).astype(np.float32))
    lows = jnp.asarray((colt[None, :] < tt[:, None]).astype(np.float32))
    lowi = jnp.asarray((colt[None, :] <= tt[:, None]).astype(np.float32))
    eye = jnp.asarray((colt[None, :] == tt[:, None]).astype(np.float32))
    bdm = (hrow[:, None] == hrow[None, :]).astype(np.float32)
    bd = jnp.asarray(bdm)
    avg = jnp.asarray(bdm / n)
    blk = pl.BlockSpec((ch, gw), lambda bi, gi, ci: (bi * nc + ci, gi))
    gvec = pl.BlockSpec((1, gw), lambda bi, gi, ci: (0, gi))
    const = lambda shape: pl.BlockSpec(shape, lambda bi, gi, ci: (0, 0))
    return pl.pallas_call(
        functools.partial(_rwkv_scan_kernel, mxu_dtype=mxu_dtype),
        grid=(bsz, c // gw, nc),
        in_specs=[blk] * 7 + [gvec] * 3 + [const((ch, ch)), const((ch, gw)), const((ch, gw)), const((ch, gw)),
                                           const((gw, gw)), const((gw, gw))],
        out_specs=blk,
        out_shape=jax.ShapeDtypeStruct((t, c), BF16),
        scratch_shapes=[pltpu.VMEM((gw, gw), F32)],
        compiler_params=_params("parallel", "parallel", "arbitrary"),
        name="rwkv_scan",
    )(r, lw, k, v, a, b, g, f(r_k), f(lnx_g), f(lnx_b), tri, lows, lowi, eye, bd, avg)


PEER_PAIRS = [(p, q) for p in range(PEER_TOPK) for q in range(PEER_TOPK // (p + 1))]
PEER_CAND_ROWS = -(-len(PEER_PAIRS) // 8) * 8


def _peer_kernel(x_ref, g_ref, wq_ref, k1_ref, k2_ref, u_ref, vt_ref, o_ref,
                 h2_ref, s1_ref, e1_ref, s2_ref, e2_ref, tau_ref, acc_ref, top_ref, cand_ref):
    e = pl.program_id(1)
    nk = PEER_KEYS
    half = PEER_KEY_DIM // 2
    eb = u_ref.shape[0]
    tt = x_ref.shape[0]

    @pl.when(e == 0)
    def _():
        hb = _rms(x_ref[...], g_ref[...]).astype(BF16)
        h2_ref[...] = hb
        q = _dot(hb, wq_ref[...])
        cand_ref[...] = jnp.full_like(cand_ref, -jnp.inf)
        for h in range(PEER_HEADS):
            base = h * PEER_KEY_DIM
            s1 = _dot_nt(k1_ref[...], q[:, base:base + half], HIGHEST)
            s2 = _dot_nt(k2_ref[...], q[:, base + half:base + 2 * half], HIGHEST)
            for side, s in enumerate((s1, s2)):
                cur = s
                for rnk in range(PEER_TOPK):
                    m = jnp.max(cur, axis=0, keepdims=True)
                    top_ref[side * PEER_TOPK + rnk:side * PEER_TOPK + rnk + 1, :] = m
                    cur = jnp.where(cur == m, -jnp.inf, cur)
            for ci, (p, qq) in enumerate(PEER_PAIRS):
                cand_ref[ci:ci + 1, :] = top_ref[p:p + 1, :] + top_ref[PEER_TOPK + qq:PEER_TOPK + qq + 1, :]
            cand = cand_ref[...]
            cur = cand
            best = jnp.max(cur, axis=0, keepdims=True)
            tau = best
            for _ in range(PEER_TOPK - 1):
                cur = jnp.where(cur == tau, -jnp.inf, cur)
                tau = jnp.max(cur, axis=0, keepdims=True)
            zsum = jnp.sum(jnp.where(cand >= tau, jnp.exp(cand - best), 0.0), axis=0, keepdims=True)
            s1_ref[h] = s1
            e1_ref[h] = jnp.exp(s1 - top_ref[0:1, :]) / zsum
            s2_ref[h] = s2
            e2_ref[h] = jnp.exp(s2 - top_ref[PEER_TOPK:PEER_TOPK + 1, :])
            tau_ref[h:h + 1, :] = tau
        acc_ref[...] = jnp.zeros_like(acc_ref)

    hu = _dot_nt(u_ref[...], h2_ref[...])
    act = 0.5 * hu * (1.0 + lax.erf(hu * (2.0 ** -0.5)))
    parts = []
    for ib in range(eb // nk):
        i = e * (eb // nk) + ib
        w = jnp.zeros((nk, tt), F32)
        for h in range(PEER_HEADS):
            cond = (s2_ref[h] + s1_ref[h, pl.ds(i, 1), :]) >= tau_ref[h:h + 1, :]
            w = w + jnp.where(cond, e2_ref[h] * e1_ref[h, pl.ds(i, 1), :], 0.0)
        parts.append((w * act[ib * nk:(ib + 1) * nk, :]).astype(BF16))
    acc_ref[...] += _dot(vt_ref[...], jnp.concatenate(parts, axis=0))

    @pl.when(e == pl.num_programs(1) - 1)
    def _():
        o_ref[...] = x_ref[...] + acc_ref[...].T


def peer(x, g, wq, k1, k2, u, vt, tt=256, eb=512):
    t, d = x.shape
    ne = u.shape[0]
    nh, nk = PEER_HEADS, PEER_KEYS
    return pl.pallas_call(
        _peer_kernel,
        grid=(t // tt, ne // eb),
        in_specs=[pl.BlockSpec((tt, d), lambda i, e: (i, 0)),
                  pl.BlockSpec((1, d), lambda i, e: (0, 0)),
                  pl.BlockSpec((d, nh * PEER_KEY_DIM), lambda i, e: (0, 0)),
                  pl.BlockSpec((nk, PEER_KEY_DIM // 2), lambda i, e: (0, 0)),
                  pl.BlockSpec((nk, PEER_KEY_DIM // 2), lambda i, e: (0, 0)),
                  pl.BlockSpec((eb, d), lambda i, e: (e, 0)),
                  pl.BlockSpec((d, eb), lambda i, e: (0, e))],
        out_specs=pl.BlockSpec((tt, d), lambda i, e: (i, 0)),
        out_shape=jax.ShapeDtypeStruct((t, d), F32),
        scratch_shapes=[pltpu.VMEM((tt, d), BF16),
                        pltpu.VMEM((nh, nk, tt), F32), pltpu.VMEM((nh, nk, tt), F32),
                        pltpu.VMEM((nh, nk, tt), F32), pltpu.VMEM((nh, nk, tt), F32),
                        pltpu.VMEM((nh, tt), F32), pltpu.VMEM((d, tt), F32),
                        pltpu.VMEM((2 * PEER_TOPK, tt), F32), pltpu.VMEM((PEER_CAND_ROWS, tt), F32)],
        compiler_params=_params("parallel", "arbitrary"),
        name="peer",
    )(x, g.astype(F32).reshape(1, d), wq, k1.astype(F32), k2.astype(F32), u, vt)


def kernel(x, rel_bias, norm1_g, w_in, conv_w, conv_b, dt_bias, a_log, d_skip, ssd_norm_g, q_norm_g, k_norm_g, rwkv_mu, w0, w_w2, a0, w_a2, w_g2, k_k, k_a, r_k, lnx_g, lnx_b, p_ssd, p_moba, p_rwkv, w_gate, b_gate, w_out, norm2_g, peer_wq, peer_k1, peer_k2, peer_u, peer_v):
    bsz, slen, d = x.shape
    t = bsz * slen
    xf = x.astype(F32).reshape(t, d)
    bias_table = moba_bias_table(rel_bias)
    far_bias = rel_bias[REL_BUCKETS - 1]
    for l in range(w_in.shape[0]):
        g1 = norm1_g[l].astype(F32)
        w_moba = w_in[l][:, IN_SSD:IN_SSD + IN_MOBA].astype(BF16)
        w_rwkv = w_in[l][:, IN_SSD + IN_MOBA:].astype(BF16)
        ya = ssd(norm_matmul(xf, g1, _ssd_weight(w_in[l])), bsz, slen, conv_w[l], conv_b[l], dt_bias[l],
                 a_log[l], d_skip[l], ssd_norm_g[l])
        yb = moba(norm_matmul(xf, g1, w_moba), bsz, slen, q_norm_g[l], k_norm_g[l], bias_table, far_bias)
        yc = rwkv(norm_matmul(xf, g1, w_rwkv), bsz, slen, rwkv_mu[l], w0[l], w_w2[l], a0[l], w_a2[l], w_g2[l],
                  k_k[l], k_a[l], r_k[l], lnx_g[l], lnx_b[l])
        xf = merge(xf, ya, yb, yc, g1, w_gate[l].astype(BF16), b_gate[l].astype(F32), p_ssd[l].astype(BF16),
                   p_moba[l].astype(BF16), p_rwkv[l].astype(BF16), w_out[l].astype(BF16))
        xf = peer(xf, norm2_g[l], peer_wq[l].astype(BF16), peer_k1[l], peer_k2[l], peer_u[l].astype(BF16),
                  peer_v[l].astype(BF16).T)
    return xf.reshape(bsz, slen, d).astype(x.dtype)
```

```python
import functools
import math

import numpy as np
import jax
import jax.numpy as jnp
from jax import lax
from jax.experimental import pallas as pl
from jax.experimental.pallas import tpu as pltpu

F32 = jnp.float32
BF16 = jnp.bfloat16
HIGHEST = lax.Precision.HIGHEST

D_MODEL = 1024
NORM_EPS = 1e-6
NEG = -1e30
VMEM_LIMIT = 56 * 1024 * 1024

SSD_HEADS = 16
SSD_HEAD_DIM = 64
SSD_INNER = 1024
SSD_GROUPS = 4
SSD_STATE = 128
SSD_CONV = 4
SSD_CHUNK = 128
SSD_XBC = SSD_INNER + 2 * SSD_GROUPS * SSD_STATE
DT_PAD = 128

MOBA_HEADS = 8
MOBA_HEAD_DIM = 128
MOBA_INNER = 1024
MOBA_BLOCK = 256
MOBA_TOPK = 3
REL_BUCKETS = 32
REL_MAX_DIST = 1024
MOBA_NEAR = 5
MOBA_QBLOCKS = 2
LOG2E = math.log2(math.e)

RWKV_HEADS = 16
RWKV_HEAD_DIM = 64
RWKV_INNER = 1024
RWKV_DECAY_LORA = 64
RWKV_ICLR_LORA = 64
RWKV_GATE_LORA = 128
RWKV_IN = 3 * RWKV_INNER + RWKV_DECAY_LORA + RWKV_ICLR_LORA + RWKV_GATE_LORA
RWKV_LN_EPS = 64e-5
RWKV_CHUNK = 64
RWKV_GROUP = 4

PEER_HEADS = 8
PEER_KEYS = 128
PEER_TOPK = 16
PEER_KEY_DIM = 256

IN_SSD = SSD_INNER + SSD_XBC + SSD_HEADS
IN_MOBA = 3 * MOBA_INNER


def _dot(a, b, precision=None):
    return jnp.dot(a, b, preferred_element_type=F32, precision=precision)


def _dot_nt(a, b, precision=None):
    return lax.dot_general(a, b, (((1,), (1,)), ((), ())), preferred_element_type=F32, precision=precision)


def _dot_tn(a, b, precision=None):
    return lax.dot_general(a, b, (((0,), (0,)), ((), ())), preferred_element_type=F32, precision=precision)


def _rms(x, g):
    return x * lax.rsqrt(jnp.mean(x * x, axis=-1, keepdims=True) + NORM_EPS) * g


def _silu(x):
    return x * jax.nn.sigmoid(x)


def _params(*sem):
    return pltpu.CompilerParams(dimension_semantics=sem, vmem_limit_bytes=VMEM_LIMIT)


def _norm_matmul_kernel(x_ref, g_ref, w_ref, o_ref, *, col_chunk):
    hb = _rms(x_ref[...], g_ref[...]).astype(BF16)
    for c in range(0, o_ref.shape[1], col_chunk):
        o_ref[:, c:c + col_chunk] = _dot(hb, w_ref[:, c:c + col_chunk])


def norm_matmul(x, g, w, tm=512, col_chunk=512):
    t, d = x.shape
    n = w.shape[1]
    col_chunk = math.gcd(n, col_chunk)
    return pl.pallas_call(
        functools.partial(_norm_matmul_kernel, col_chunk=col_chunk),
        grid=(t // tm,),
        in_specs=[pl.BlockSpec((tm, d), lambda i: (i, 0)),
                  pl.BlockSpec((1, d), lambda i: (0, 0)),
                  pl.BlockSpec((d, n), lambda i: (0, 0))],
        out_specs=pl.BlockSpec((tm, n), lambda i: (i, 0)),
        out_shape=jax.ShapeDtypeStruct((t, n), F32),
        compiler_params=_params("parallel"),
        name="norm_matmul",
    )(x, g.reshape(1, d), w)


def _merge_kernel(x_ref, ya_ref, yb_ref, yc_ref, g_ref, wg_ref, bg_ref, pa_ref, pb_ref, pc_ref, wo_ref, o_ref):
    x = x_ref[...]
    hb = _rms(x, g_ref[...]).astype(BF16)
    d = x.shape[1]
    merged = None
    for i, (y_ref, p_ref) in enumerate(((ya_ref, pa_ref), (yb_ref, pb_ref), (yc_ref, pc_ref))):
        gate = jax.nn.sigmoid(_dot(hb, wg_ref[:, i * d:(i + 1) * d]) + bg_ref[:, i * d:(i + 1) * d])
        term = gate * _dot(y_ref[...], p_ref[...])
        merged = term if merged is None else merged + term
    o_ref[...] = x + _dot(merged.astype(BF16), wo_ref[...])


def merge(x, ya, yb, yc, g, w_gate, b_gate, pa, pb, pc, w_out, tm=256):
    t, d = x.shape
    row = lambda i: (i, 0)
    fixed = lambda i: (0, 0)
    return pl.pallas_call(
        _merge_kernel,
        grid=(t // tm,),
        in_specs=[pl.BlockSpec((tm, d), row), pl.BlockSpec((tm, d), row), pl.BlockSpec((tm, d), row),
                  pl.BlockSpec((tm, d), row), pl.BlockSpec((1, d), fixed),
                  pl.BlockSpec((d, 3 * d), fixed), pl.BlockSpec((1, 3 * d), fixed),
                  pl.BlockSpec((d, d), fixed), pl.BlockSpec((d, d), fixed), pl.BlockSpec((d, d), fixed),
                  pl.BlockSpec((d, d), fixed)],
        out_specs=pl.BlockSpec((tm, d), row),
        out_shape=jax.ShapeDtypeStruct((t, d), F32),
        compiler_params=_params("parallel"),
        name="merge",
    )(x, ya, yb, yc, g.reshape(1, d), w_gate, b_gate.reshape(1, 3 * d), pa, pb, pc, w_out)


def _ssd_kernel(xbc_ref, z_ref, dt_ref, cw_ref, cb_ref, dtb_ref, a_ref, dsk_ref, ng_ref, rexp_ref, tri_ref,
                o_ref, buf_ref, state_ref):
    lc = SSD_CHUNK
    c = pl.program_id(1)

    @pl.when(c == 0)
    def _():
        buf_ref[0:8, :] = jnp.zeros((8, SSD_XBC), F32)
        state_ref[...] = jnp.zeros_like(state_ref)

    cur = xbc_ref[...]
    buf_ref[8:8 + lc, :] = cur
    acc = cb_ref[...] + cw_ref[0:1, :] * buf_ref[5:5 + lc, :]
    for k in range(1, SSD_CONV):
        acc = acc + cw_ref[k:k + 1, :] * buf_ref[5 + k:5 + k + lc, :]
    buf_ref[0:8, :] = cur[lc - 8:lc, :]
    xbc = _silu(acc)
    xs = xbc[:, :SSD_INNER]

    dt = jax.nn.softplus(dt_ref[...] + dtb_ref[...])
    acs = _dot(tri_ref[...], dt * a_ref[...], HIGHEST)
    acs_t = acs.T
    rexp = rexp_ref[...]
    dt_x = _dot(dt, rexp, HIGHEST)
    e_x = _dot(jnp.exp(acs), rexp, HIGHEST)
    dte_x = _dot(jnp.exp(acs[lc - 1:lc, :] - acs), rexp, HIGHEST)
    xdt = xs * dt_x
    xdt_b = xdt.astype(BF16)
    xw_b = (xdt * dte_x).astype(BF16)
    chunk_decay = e_x[lc - 1:lc, :]

    row = lax.broadcasted_iota(jnp.int32, (lc, lc), 0)
    col = lax.broadcasted_iota(jnp.int32, (lc, lc), 1)
    causal = row >= col
    gw = SSD_INNER // SSD_GROUPS
    hpg = SSD_HEADS // SSD_GROUPS
    n = SSD_STATE
    ys = []
    for g in range(SSD_GROUPS):
        b_g = xbc[:, SSD_INNER + g * n:SSD_INNER + (g + 1) * n].astype(BF16)
        c_g = xbc[:, SSD_INNER + (SSD_GROUPS + g) * n:SSD_INNER + (SSD_GROUPS + g + 1) * n].astype(BF16)
        cb = _dot_nt(c_g, b_g)
        s_in = state_ref[:, g * gw:(g + 1) * gw]
        y_off = _dot(c_g, s_in.astype(BF16)) * e_x[:, g * gw:(g + 1) * gw]
        state_ref[:, g * gw:(g + 1) * gw] = (chunk_decay[:, g * gw:(g + 1) * gw] * s_in
                                             + _dot_tn(b_g, xw_b[:, g * gw:(g + 1) * gw]))
        diag = []
        for e in range(hpg):
            h = g * hpg + e
            seg = jnp.exp(jnp.where(causal, acs[:, h:h + 1] - acs_t[h:h + 1, :], NEG))
            m = (cb * seg).astype(BF16)
            diag.append(_dot(m, xdt_b[:, h * SSD_HEAD_DIM:(h + 1) * SSD_HEAD_DIM]))
        ys.append(jnp.concatenate(diag, axis=-1) + y_off)
    y = jnp.concatenate(ys, axis=-1) + xs * dsk_ref[...]
    y = y * _silu(z_ref[...])
    for g in range(SSD_GROUPS):
        yg = y[:, g * gw:(g + 1) * gw]
        yg = yg * lax.rsqrt(jnp.mean(yg * yg, axis=-1, keepdims=True) + NORM_EPS)
        o_ref[:, g * gw:(g + 1) * gw] = (yg * ng_ref[:, g * gw:(g + 1) * gw]).astype(o_ref.dtype)


def ssd(proj, bsz, slen, conv_w, conv_b, dt_bias, a_log, d_skip, norm_g):
    lc = SSD_CHUNK
    nc = slen // lc
    t = bsz * slen
    pad = DT_PAD - SSD_HEADS
    dtb = jnp.pad(dt_bias.astype(F32), (0, pad)).reshape(1, DT_PAD)
    a = jnp.pad(-jnp.exp(a_log.astype(F32)), (0, pad)).reshape(1, DT_PAD)
    dsk = jnp.repeat(d_skip.astype(F32), SSD_HEAD_DIM).reshape(1, SSD_INNER)
    rexp = jnp.asarray((np.arange(DT_PAD)[:, None] == np.arange(SSD_INNER)[None, :] // SSD_HEAD_DIM)
                       .astype(np.float32))
    tri = jnp.asarray(np.tril(np.ones((lc, lc), np.float32)))
    rowmap = lambda b, c: (b * nc + c, 0)
    fixed = lambda b, c: (0, 0)
    zblk = SSD_XBC // SSD_INNER
    dtblk = (SSD_XBC + SSD_INNER) // DT_PAD
    return pl.pallas_call(
        _ssd_kernel,
        grid=(bsz, nc),
        in_specs=[pl.BlockSpec((lc, SSD_XBC), rowmap),
                  pl.BlockSpec((lc, SSD_INNER), lambda b, c: (b * nc + c, zblk)),
                  pl.BlockSpec((lc, DT_PAD), lambda b, c: (b * nc + c, dtblk)),
                  pl.BlockSpec((SSD_CONV, SSD_XBC), fixed), pl.BlockSpec((1, SSD_XBC), fixed),
                  pl.BlockSpec((1, DT_PAD), fixed), pl.BlockSpec((1, DT_PAD), fixed),
                  pl.BlockSpec((1, SSD_INNER), fixed), pl.BlockSpec((1, SSD_INNER), fixed),
                  pl.BlockSpec((DT_PAD, SSD_INNER), fixed), pl.BlockSpec((lc, lc), fixed)],
        out_specs=pl.BlockSpec((lc, SSD_INNER), rowmap),
        out_shape=jax.ShapeDtypeStruct((t, SSD_INNER), BF16),
        scratch_shapes=[pltpu.VMEM((lc + 8, SSD_XBC), F32), pltpu.VMEM((SSD_STATE, SSD_INNER), F32)],
        compiler_params=_params("parallel", "arbitrary"),
        name="ssd",
    )(proj, proj, proj, conv_w.astype(F32), conv_b.astype(F32).reshape(1, SSD_XBC), dtb, a, dsk,
      norm_g.astype(F32).reshape(1, SSD_INNER), rexp, tri)


def _ssd_weight(w_in_l):
    w = w_in_l[:, :IN_SSD]
    z, xbc, dt = w[:, :SSD_INNER], w[:, SSD_INNER:SSD_INNER + SSD_XBC], w[:, SSD_INNER + SSD_XBC:]
    dt = jnp.pad(dt, ((0, 0), (0, DT_PAD - SSD_HEADS)))
    return jnp.concatenate([xbc, z, dt], axis=1).astype(BF16)


def _t5_bucket(dist):
    max_exact = REL_BUCKETS // 2
    d = jnp.maximum(dist, 0)
    df = jnp.maximum(d, 1).astype(F32)
    large = max_exact + (jnp.log(df / max_exact) / math.log(REL_MAX_DIST / max_exact)
                         * (REL_BUCKETS - max_exact)).astype(jnp.int32)
    large = jnp.minimum(large, REL_BUCKETS - 1)
    return jnp.where(d < max_exact, d, large)


def _bias_table_kernel(bias_ref, bucket_ref, o_ref):
    h = pl.program_id(0)
    for d in range(MOBA_NEAR):
        bk = bucket_ref[d]
        acc = jnp.where(bk < 0, NEG, 0.0)
        for b in range(REL_BUCKETS):
            acc = jnp.where(bk == b, bias_ref[h, b] * LOG2E, acc)
        o_ref[d] = acc
    o_ref[MOBA_NEAR] = jnp.zeros(o_ref.shape[1:], F32)


def moba_bias_table(rel_bias):
    blk = MOBA_BLOCK
    q = jnp.arange(blk, dtype=jnp.int32)[None, :, None]
    k = jnp.arange(blk, dtype=jnp.int32)[None, None, :]
    dist = jnp.arange(MOBA_NEAR, dtype=jnp.int32)[:, None, None] * blk + q - k
    bucket = jnp.where(dist < 0, -1, _t5_bucket(dist))
    return pl.pallas_call(
        _bias_table_kernel,
        grid=(MOBA_HEADS,),
        in_specs=[pl.BlockSpec(memory_space=pltpu.SMEM),
                  pl.BlockSpec((MOBA_NEAR, blk, blk), lambda h: (0, 0, 0))],
        out_specs=pl.BlockSpec((None, MOBA_NEAR + 1, blk, blk), lambda h: (h, 0, 0, 0)),
        out_shape=jax.ShapeDtypeStruct((MOBA_HEADS, MOBA_NEAR + 1, blk, blk), F32),
        compiler_params=_params("arbitrary"),
        name="moba_bias_table",
    )(rel_bias.astype(F32).T, bucket)


def _moba_prep_kernel(q_ref, k_ref, v_ref, qg_ref, kg_ref, far_ref, qa_ref, ka_ref, va_ref, kmean_ref):
    i = pl.program_id(2)
    blk, dh = q_ref.shape
    half = dh // 2

    @pl.when(i == 0)
    def _():
        kmean_ref[...] = jnp.zeros_like(kmean_ref)

    qn = _rms(q_ref[...], qg_ref[...])
    kn = _rms(k_ref[...], kg_ref[...])
    gate = _dot_nt(qn, kmean_ref[...], HIGHEST)
    col = lax.broadcasted_iota(jnp.int32, (blk, dh), 1)
    colf = col.astype(F32)
    valid = col < i
    g = jnp.where(valid, gate, NEG)
    sel = col < 0
    for _ in range(MOBA_TOPK):
        m = jnp.max(g, axis=-1, keepdims=True)
        idx = jnp.min(jnp.where(g == m, colf, float(dh)), axis=-1, keepdims=True)
        hit = colf == idx
        sel = sel | hit
        g = jnp.where(hit, -jnp.inf, g)
    sel = sel & valid

    far = far_ref[...] * LOG2E
    far_hi = far.astype(BF16).astype(F32)
    far_lo = far - far_hi
    sel_f = jnp.where(sel, 1.0, 0.0)
    low = col >= half
    sel2 = jnp.where(low, pltpu.roll(sel_f, half, 1), sel_f) > 0.5
    jcol = jnp.where(low, col - half, col)
    is_far = (i - jcol) >= MOBA_NEAR
    pen_hi = jnp.where(jcol == i, 0.0, jnp.where(sel2, jnp.where(is_far, far_hi, 0.0), NEG))
    pen_lo = jnp.where(sel2 & is_far, far_lo, 0.0)
    qa_ref[:, 0:dh] = (qn * (dh ** -0.5 * LOG2E)).astype(BF16)
    qa_ref[:, dh:2 * dh] = jnp.where(low, pen_lo, pen_hi).astype(BF16)
    ka_ref[:, 0:dh] = kn.astype(BF16)
    ka_ref[:, dh:2 * dh] = jnp.where(jcol == i, 1.0, 0.0).astype(BF16)
    va_ref[:, 0:dh] = v_ref[...].astype(BF16)
    va_ref[:, dh:2 * dh] = jnp.ones((blk, dh), BF16)
    kmean_ref[pl.ds(i, 1), :] = jnp.mean(kn, axis=0, keepdims=True)


def _moba_attn_kernel(qa_ref, ka_ref, va_ref, tab_ref, o_ref, m_ref, acc_ref):
    a = pl.program_id(2)
    blk = MOBA_BLOCK
    dh = m_ref.shape[1]
    q = qa_ref[...]
    m_ref[...] = jnp.full_like(m_ref, 0.1 * NEG)
    acc_ref[...] = jnp.zeros_like(acc_ref)

    def step(off, nk, tab):
        s = _dot_nt(q, ka_ref[pl.ds(off, nk), :])
        if tab is not None:
            s = s + tab
        m_old = m_ref[...]
        m_new = jnp.maximum(m_old, jnp.max(s, axis=-1, keepdims=True))
        alpha = jnp.exp2(m_old - m_new)
        p = jnp.exp2(s - jnp.concatenate([m_new] * (nk // dh), axis=-1)).astype(BF16)
        acc_ref[...] = (jnp.concatenate([alpha, alpha], axis=-1) * acc_ref[...]
                        + _dot(p, va_ref[pl.ds(off, nk), :]))
        m_ref[...] = m_new

    first = MOBA_QBLOCKS * a
    n_far = jnp.maximum(first - (MOBA_NEAR - 1), 0)

    def far_body(t, c):
        step(pl.multiple_of(t * (MOBA_QBLOCKS * blk), MOBA_QBLOCKS * blk), MOBA_QBLOCKS * blk, None)
        return c

    def near_body(j, c):
        tab = jnp.concatenate([tab_ref[jnp.clip(first + r - j, 0, MOBA_NEAR)] for r in range(MOBA_QBLOCKS)], axis=0)
        step(pl.multiple_of(j * blk, blk), blk, tab)
        return c

    lax.fori_loop(0, n_far // MOBA_QBLOCKS, far_body, 0)
    lax.fori_loop(n_far, first + MOBA_QBLOCKS, near_body, 0)
    acc = acc_ref[...]
    o_ref[...] = (acc[:, :dh] / acc[:, dh:]).astype(o_ref.dtype)


def moba(proj, bsz, slen, q_norm_g, k_norm_g, bias_table, far_bias):
    blk, dh, nh = MOBA_BLOCK, MOBA_HEAD_DIM, MOBA_HEADS
    nb = slen // blk
    tq = MOBA_QBLOCKS * blk
    assert slen % tq == 0 and nb <= dh // 2 and (MOBA_NEAR - 1) % MOBA_QBLOCKS == 0
    t = bsz * slen
    far = jnp.broadcast_to(far_bias.astype(F32)[:, None, None], (nh, 1, dh))
    aug = pl.BlockSpec((None, None, blk, 2 * dh), lambda b, h, i: (b, h, i, 0))
    qa, ka, va = pl.pallas_call(
        _moba_prep_kernel,
        grid=(bsz, nh, nb),
        in_specs=[pl.BlockSpec((blk, dh), lambda b, h, i: (b * nb + i, h)),
                  pl.BlockSpec((blk, dh), lambda b, h, i: (b * nb + i, nh + h)),
                  pl.BlockSpec((blk, dh), lambda b, h, i: (b * nb + i, 2 * nh + h)),
                  pl.BlockSpec((1, dh), lambda b, h, i: (0, 0)),
                  pl.BlockSpec((1, dh), lambda b, h, i: (0, 0)),
                  pl.BlockSpec((None, 1, dh), lambda b, h, i: (h, 0, 0))],
        out_specs=[aug, aug, aug],
        out_shape=[jax.ShapeDtypeStruct((bsz, nh, slen, 2 * dh), BF16)] * 3,
        scratch_shapes=[pltpu.VMEM((dh, dh), F32)],
        compiler_params=_params("parallel", "parallel", "arbitrary"),
        name="moba_prep",
    )(proj, proj, proj, q_norm_g.astype(F32).reshape(1, dh), k_norm_g.astype(F32).reshape(1, dh), far)
    whole = pl.BlockSpec((None, None, slen, 2 * dh), lambda b, h, a: (b, h, 0, 0))
    return pl.pallas_call(
        _moba_attn_kernel,
        grid=(bsz, nh, slen // tq),
        in_specs=[pl.BlockSpec((None, None, tq, 2 * dh), lambda b, h, a: (b, h, a, 0)), whole, whole,
                  pl.BlockSpec((None, MOBA_NEAR + 1, blk, blk), lambda b, h, a: (h, 0, 0, 0))],
        out_specs=pl.BlockSpec((tq, dh), lambda b, h, a: (b * (slen // tq) + a, h)),
        out_shape=jax.ShapeDtypeStruct((t, nh * dh), BF16),
        scratch_shapes=[pltpu.VMEM((tq, dh), F32), pltpu.VMEM((tq, 2 * dh), F32)],
        compiler_params=_params("parallel", "parallel", "arbitrary"),
        name="moba_attn",
    )(qa, ka, va, bias_table)


def _rwkv_prep_kernel(cur_ref, prev_ref, mu_ref, w0_ref, ww2_ref, a0_ref, wa2_ref, wg2_ref, kk_ref, ka_ref,
                      hsum_ref, hexp_ref, r_ref, lw_ref, k_ref, v_ref, a_ref, b_ref, g_ref, buf_ref,
                      *, tiles_per_seq):
    i = pl.program_id(0)
    tm = cur_ref.shape[0]
    c = RWKV_INNER
    cur = cur_ref[...]
    first = (i % tiles_per_seq) == 0
    buf_ref[8:8 + tm, :] = cur
    buf_ref[0:8, :] = jnp.where(first, 0.0, prev_ref[...])
    prev = buf_ref[7:7 + tm, :]
    xs = cur + (prev - cur) * mu_ref[...]
    r, k, v = xs[:, :c], xs[:, c:2 * c], xs[:, 2 * c:3 * c]
    o1 = 3 * c + RWKV_DECAY_LORA
    o2 = o1 + RWKV_ICLR_LORA
    wl, al, gl = xs[:, 3 * c:o1], xs[:, o1:o2], xs[:, o2:]
    w = -jax.nn.softplus(-(w0_ref[...] + _dot(jnp.tanh(wl), ww2_ref[...], HIGHEST))) - 0.5
    a = jax.nn.sigmoid(a0_ref[...] + _dot(al, wa2_ref[...], HIGHEST))
    kk = k * kk_ref[...]
    ss = _dot(_dot(kk * kk, hsum_ref[...], HIGHEST), hexp_ref[...], HIGHEST)
    kk = kk * lax.rsqrt(jnp.maximum(ss, 1e-12))
    r_ref[...] = r
    lw_ref[...] = -jnp.exp(w)
    k_ref[...] = k * (1.0 + (a - 1.0) * ka_ref[...])
    v_ref[...] = v
    a_ref[...] = -kk
    b_ref[...] = kk * a
    g_ref[...] = _dot(jax.nn.sigmoid(gl), wg2_ref[...], HIGHEST)


def _bd_rows(y, mask):
    return jnp.where(mask, jnp.concatenate([y] * RWKV_GROUP, axis=0), 0.0)


def _rwkv_scan_kernel(r_ref, lw_ref, k_ref, v_ref, a_ref, b_ref, g_ref, rk_ref, lng_ref, lnb_ref,
                      tri_ref, lows_ref, lowi_ref, eye_ref, bd_ref, avg_ref, o_ref, s_ref, *, mxu_dtype):
    ci = pl.program_id(2)
    ch = RWKV_CHUNK

    @pl.when(ci == 0)
    def _():
        s_ref[...] = jnp.zeros_like(s_ref)

    prec = HIGHEST if mxu_dtype == F32 else None

    def mm(x, y):
        return _dot(x.astype(mxu_dtype), y.astype(mxu_dtype), prec)

    def mm_nt(x, y):
        return _dot_nt(x.astype(mxu_dtype), y.astype(mxu_dtype), prec)

    def mm_tn(x, y):
        return _dot_tn(x.astype(mxu_dtype), y.astype(mxu_dtype), prec)

    bd = bd_ref[...] > 0.5
    strict = lows_ref[...] > 0.5
    incl = lowi_ref[...] > 0.5
    gw = bd_ref.shape[0]
    eye_big = lax.broadcasted_iota(jnp.int32, (gw, gw), 0) == lax.broadcasted_iota(jnp.int32, (gw, gw), 1)
    avg = avg_ref[...]

    ng = s_ref.shape[0]
    cols = [slice(gi * gw, (gi + 1) * gw) for gi in range(ng)]

    def each(fn, *lists):
        return [fn(*xs) for xs in zip(*lists)]

    def bdr(ys):
        return [_bd_rows(y, bd) for y in ys]

    r, lw, k, v, a, b = ([ref[:, c] for c in cols] for ref in (r_ref, lw_ref, k_ref, v_ref, a_ref, b_ref))
    cl = each(lambda x: _dot(tri_ref[...], x, HIGHEST), lw)
    cl_end = [x[ch - 1:ch, :] for x in cl]
    e_neg = [jnp.exp(-x) for x in cl]
    a_t = each(lambda x, c, l: x * jnp.exp(c - l), a, cl, lw)
    r_t = each(lambda x, c: x * jnp.exp(c), r, cl)
    b_t = each(jnp.multiply, b, e_neg)
    k_t = each(jnp.multiply, k, e_neg)
    e_end = each(lambda ce, c: jnp.exp(ce - c), cl_end, cl)
    b_h = each(jnp.multiply, b, e_end)
    k_h = each(jnp.multiply, k, e_end)

    ar = each(lambda x, y: jnp.concatenate([x, y], axis=0), a_t, r_t)
    gb = each(mm_nt, ar, bdr(b_t))
    gk = each(mm_nt, ar, bdr(k_t))
    a_ab = [jnp.where(strict, x[:ch], 0.0) for x in gb]
    a_ak = [jnp.where(strict, x[:ch], 0.0) for x in gk]
    q_rb = [jnp.where(incl, x[ch:], 0.0) for x in gb]
    q_rk = [jnp.where(incl, x[ch:], 0.0) for x in gk]

    t_inv = [eye_ref[...] + x for x in a_ab]
    p = a_ab
    for _ in range(int(math.log2(ch)) - 1):
        p = each(mm, p, bdr(p))
        t_inv = each(lambda t, tp: t + tp, t_inv, each(mm, t_inv, bdr(p)))

    vbd = bdr(v)
    w_t = each(mm, t_inv, bdr(a_t))
    u0 = each(mm, t_inv, bdr(each(mm, a_ak, vbd)))
    z = each(lambda x, y: x + y, r_t, each(mm, q_rb, bdr(w_t)))
    y_loc = each(lambda x, y: x + y, each(mm, q_rb, bdr(u0)), each(mm, q_rk, vbd))
    bwu = each(lambda x, w, u: mm_tn(x, jnp.concatenate([w, u], axis=1)), b_h, w_t, u0)
    kv = each(mm_tn, k_h, v)
    m_s = each(lambda x, ce: jnp.where(bd, x[:, :gw], 0.0) + jnp.where(eye_big, jnp.exp(ce), 0.0), bwu, cl_end)
    n_s = each(lambda x, y: jnp.where(bd, x[:, gw:] + y, 0.0), bwu, kv)
    s0 = [s_ref[gi] for gi in range(ng)]
    zs = each(lambda x, m, s: mm(jnp.concatenate([x, m], axis=0), s), z, m_s, s0)
    y = each(lambda x, yl: x[:ch] + yl, zs, y_loc)
    for gi in range(ng):
        s_ref[gi] = zs[gi][ch:] + n_s[gi]

    mean = [_dot(x, avg, HIGHEST) for x in y]
    dlt = each(lambda x, m: x - m, y, mean)
    var = [_dot(x * x, avg, HIGHEST) for x in dlt]
    rkk = each(lambda x, kk, c: _dot(x * kk * rk_ref[:, c], avg, HIGHEST), r, k, cols)
    for gi, c in enumerate(cols):
        yn = dlt[gi] * lax.rsqrt(var[gi] + RWKV_LN_EPS) * lng_ref[:, c] + lnb_ref[:, c]
        bonus = rkk[gi] * float(RWKV_HEAD_DIM) * v[gi]
        o_ref[:, c] = ((yn + bonus) * g_ref[:, c]).astype(o_ref.dtype)


def rwkv(proj, bsz, slen, mu, w0, w_w2, a0, w_a2, w_g2, k_k, k_a, r_k, lnx_g, lnx_b, tm=256, mxu_dtype=BF16,
         groups_per_step=4):
    c, ch, grp, n = RWKV_INNER, RWKV_CHUNK, RWKV_GROUP, RWKV_HEAD_DIM
    t = bsz * slen
    f = lambda p: p.astype(F32).reshape(1, -1)
    heads = np.arange(c) // n
    hsum = jnp.asarray((heads[:, None] == np.arange(128)[None, :]).astype(np.float32))
    hexp = hsum.T
    row = lambda i: (i, 0)
    fixed = lambda i: (0, 0)
    wide = pl.BlockSpec((tm, c), row)
    vec = pl.BlockSpec((1, c), fixed)
    r, lw, k, v, a, b, g = pl.pallas_call(
        functools.partial(_rwkv_prep_kernel, tiles_per_seq=slen // tm),
        grid=(t // tm,),
        in_specs=[pl.BlockSpec((tm, RWKV_IN), row),
                  pl.BlockSpec((8, RWKV_IN), lambda i: (jnp.maximum(i * (tm // 8) - 1, 0), 0)),
                  pl.BlockSpec((1, RWKV_IN), fixed), vec,
                  pl.BlockSpec((RWKV_DECAY_LORA, c), fixed), vec,
                  pl.BlockSpec((RWKV_ICLR_LORA, c), fixed), pl.BlockSpec((RWKV_GATE_LORA, c), fixed),
                  vec, vec, pl.BlockSpec((c, 128), fixed), pl.BlockSpec((128, c), fixed)],
        out_specs=[wide] * 7,
        out_shape=[jax.ShapeDtypeStruct((t, c), F32)] * 7,
        scratch_shapes=[pltpu.VMEM((tm + 8, RWKV_IN), F32)],
        compiler_params=_params("parallel"),
        name="rwkv_prep",
    )(proj, proj, f(mu), f(w0), w_w2.astype(F32), f(a0), w_a2.astype(F32), w_g2.astype(F32), f(k_k), f(k_a),
      hsum, hexp)

    gw = grp * n
    nc = slen // ch
    tt = np.arange(ch)
    colt = np.arange(gw) % ch
    hrow = np.arange(gw) // n
    tri = jnp.asarray((tt[:, None] >= tt[None, :]).astype(np.float32))
    lows = jnp.asarray((colt[None, :] < tt[:, None]).astype(np.float32))
    lowi = jnp.asarray((colt[None, :] <= tt[:, None]).astype(np.float32))
    eye = jnp.asarray((colt[None, :] == tt[:, None]).astype(np.float32))
    bdm = (hrow[:, None] == hrow[None, :]).astype(np.float32)
    bd = jnp.asarray(bdm)
    avg = jnp.asarray(bdm / n)
    sw = groups_per_step * gw
    blk = pl.BlockSpec((ch, sw), lambda bi, gi, ci: (bi * nc + ci, gi))
    gvec = pl.BlockSpec((1, sw), lambda bi, gi, ci: (0, gi))
    const = lambda shape: pl.BlockSpec(shape, lambda bi, gi, ci: (0, 0))
    return pl.pallas_call(
        functools.partial(_rwkv_scan_kernel, mxu_dtype=mxu_dtype),
        grid=(bsz, c // sw, nc),
        in_specs=[blk] * 7 + [gvec] * 3 + [const((ch, ch)), const((ch, gw)), const((ch, gw)), const((ch, gw)),
                                           const((gw, gw)), const((gw, gw))],
        out_specs=blk,
        out_shape=jax.ShapeDtypeStruct((t, c), BF16),
        scratch_shapes=[pltpu.VMEM((groups_per_step, gw, gw), F32)],
        compiler_params=_params("parallel", "parallel", "arbitrary"),
        name="rwkv_scan",
    )(r, lw, k, v, a, b, g, f(r_k), f(lnx_g), f(lnx_b), tri, lows, lowi, eye, bd, avg)


PEER_RANKS = PEER_TOPK + 1
PEER_PAIRS = [(p, q) for p in range(PEER_RANKS) for q in range(PEER_RANKS // (p + 1))]
PEER_CAND_ROWS = -(-len(PEER_PAIRS) // 8) * 8
PEER_TOP_ROWS = -(-2 * PEER_RANKS // 8) * 8
PEER_SUB = 256


def _peer_kernel(x_ref, g_ref, wq_ref, k1_ref, k2_ref, u_ref, vt_ref, o_ref,
                 h2_ref, thr_ref, e1_ref, s2_ref, e2_ref, acc_ref, top_ref, cand_ref):
    e = pl.program_id(1)
    nk = PEER_KEYS
    half = PEER_KEY_DIM // 2
    eb = u_ref.shape[0]

    @pl.when(e == 0)
    def _():
        hb = _rms(x_ref[...], g_ref[...]).astype(BF16)
        h2_ref[...] = hb
        q = _dot(hb, wq_ref[...])
        cand_ref[...] = jnp.full_like(cand_ref, -jnp.inf)
        for h in range(PEER_HEADS):
            base = h * PEER_KEY_DIM
            s1 = _dot_nt(k1_ref[...], q[:, base:base + half], HIGHEST)
            s2 = _dot_nt(k2_ref[...], q[:, base + half:base + 2 * half], HIGHEST)
            for side, s in enumerate((s1, s2)):
                cur = s
                for rnk in range(PEER_RANKS):
                    m = jnp.max(cur, axis=0, keepdims=True)
                    top_ref[side * PEER_RANKS + rnk:side * PEER_RANKS + rnk + 1, :] = m
                    cur = jnp.where(cur == m, -jnp.inf, cur)
            for ci, (p, qq) in enumerate(PEER_PAIRS):
                cand_ref[ci:ci + 1, :] = top_ref[p:p + 1, :] + top_ref[PEER_RANKS + qq:PEER_RANKS + qq + 1, :]
            cand = cand_ref[...]
            cur = cand
            best = jnp.max(cur, axis=0, keepdims=True)
            tau = best
            for _ in range(PEER_TOPK - 1):
                cur = jnp.where(cur == tau, -jnp.inf, cur)
                tau = jnp.max(cur, axis=0, keepdims=True)
            nxt = jnp.max(jnp.where(cur == tau, -jnp.inf, cur), axis=0, keepdims=True)
            zsum = jnp.sum(jnp.where(cand >= tau, jnp.exp(cand - best), 0.0), axis=0, keepdims=True)
            thr_ref[h] = 0.5 * (tau + nxt) - s1
            e1_ref[h] = jnp.exp(s1 - top_ref[0:1, :]) / zsum
            s2_ref[h] = s2
            e2_ref[h] = jnp.exp(s2 - top_ref[PEER_RANKS:PEER_RANKS + 1, :])
        acc_ref[...] = jnp.zeros_like(acc_ref)

    subs = [slice(sb * PEER_SUB, (sb + 1) * PEER_SUB) for sb in range(eb // PEER_SUB)]
    hus = [_dot_nt(u_ref[rows, :], h2_ref[...]) for rows in subs]
    for sb, (rows, hu) in enumerate(zip(subs, hus)):
        act = 0.5 * hu * (1.0 + lax.erf(hu * (2.0 ** -0.5)))
        parts = []
        for ib in range(PEER_SUB // nk):
            i = e * (eb // nk) + sb * (PEER_SUB // nk) + ib
            w = None
            for h in range(PEER_HEADS):
                term = jnp.where(s2_ref[h] >= thr_ref[h, pl.ds(i, 1), :],
                                 e2_ref[h] * e1_ref[h, pl.ds(i, 1), :], 0.0)
                w = term if w is None else w + term
            parts.append((w * act[ib * nk:(ib + 1) * nk, :]).astype(BF16))
        acc_ref[...] += _dot(vt_ref[:, rows], jnp.concatenate(parts, axis=0))

    @pl.when(e == pl.num_programs(1) - 1)
    def _():
        o_ref[...] = x_ref[...] + acc_ref[...].T


def peer(x, g, wq, k1, k2, u, vt, tt=512, eb=512):
    t, d = x.shape
    ne = u.shape[0]
    nh, nk = PEER_HEADS, PEER_KEYS
    return pl.pallas_call(
        _peer_kernel,
        grid=(t // tt, ne // eb),
        in_specs=[pl.BlockSpec((tt, d), lambda i, e: (i, 0)),
                  pl.BlockSpec((1, d), lambda i, e: (0, 0)),
                  pl.BlockSpec((d, nh * PEER_KEY_DIM), lambda i, e: (0, 0)),
                  pl.BlockSpec((nk, PEER_KEY_DIM // 2), lambda i, e: (0, 0)),
                  pl.BlockSpec((nk, PEER_KEY_DIM // 2), lambda i, e: (0, 0)),
                  pl.BlockSpec((eb, d), lambda i, e: (e, 0)),
                  pl.BlockSpec((d, eb), lambda i, e: (0, e))],
        out_specs=pl.BlockSpec((tt, d), lambda i, e: (i, 0)),
        out_shape=jax.ShapeDtypeStruct((t, d), F32),
        scratch_shapes=[pltpu.VMEM((tt, d), BF16),
                        pltpu.VMEM((nh, nk, tt), F32), pltpu.VMEM((nh, nk, tt), F32),
                        pltpu.VMEM((nh, nk, tt), F32), pltpu.VMEM((nh, nk, tt), F32),
                        pltpu.VMEM((d, tt), F32),
                        pltpu.VMEM((PEER_TOP_ROWS, tt), F32), pltpu.VMEM((PEER_CAND_ROWS, tt), F32)],
        compiler_params=_params("parallel", "arbitrary"),
        name="peer",
    )(x, g.astype(F32).reshape(1, d), wq, k1.astype(F32), k2.astype(F32), u, vt)


def kernel(x, rel_bias, norm1_g, w_in, conv_w, conv_b, dt_bias, a_log, d_skip, ssd_norm_g, q_norm_g, k_norm_g, rwkv_mu, w0, w_w2, a0, w_a2, w_g2, k_k, k_a, r_k, lnx_g, lnx_b, p_ssd, p_moba, p_rwkv, w_gate, b_gate, w_out, norm2_g, peer_wq, peer_k1, peer_k2, peer_u, peer_v):
    bsz, slen, d = x.shape
    t = bsz * slen
    xf = x.astype(F32).reshape(t, d)
    bias_table = moba_bias_table(rel_bias)
    far_bias = rel_bias[REL_BUCKETS - 1]
    for l in range(w_in.shape[0]):
        g1 = norm1_g[l].astype(F32)
        w_moba = w_in[l][:, IN_SSD:IN_SSD + IN_MOBA].astype(BF16)
        w_rwkv = w_in[l][:, IN_SSD + IN_MOBA:].astype(BF16)
        ya = ssd(norm_matmul(xf, g1, _ssd_weight(w_in[l])), bsz, slen, conv_w[l], conv_b[l], dt_bias[l],
                 a_log[l], d_skip[l], ssd_norm_g[l])
        yb = moba(norm_matmul(xf, g1, w_moba), bsz, slen, q_norm_g[l], k_norm_g[l], bias_table, far_bias)
        yc = rwkv(norm_matmul(xf, g1, w_rwkv), bsz, slen, rwkv_mu[l], w0[l], w_w2[l], a0[l], w_a2[l], w_g2[l],
                  k_k[l], k_a[l], r_k[l], lnx_g[l], lnx_b[l])
        xf = merge(xf, ya, yb, yc, g1, w_gate[l].astype(BF16), b_gate[l].astype(F32), p_ssd[l].astype(BF16),
                   p_moba[l].astype(BF16), p_rwkv[l].astype(BF16), w_out[l].astype(BF16))
        xf = peer(xf, norm2_g[l], peer_wq[l].astype(BF16), peer_k1[l], peer_k2[l], peer_u[l].astype(BF16),
                  peer_v[l].astype(BF16).T)
    return xf.reshape(bsz, slen, d).astype(x.dtype)
```

```python
import functools
import math

import numpy as np
import jax
import jax.numpy as jnp
from jax import lax
from jax.experimental import pallas as pl
from jax.experimental.pallas import tpu as pltpu

F32 = jnp.float32
BF16 = jnp.bfloat16
HIGHEST = lax.Precision.HIGHEST

D_MODEL = 1024
NORM_EPS = 1e-6
NEG = -1e30
VMEM_LIMIT = 56 * 1024 * 1024

SSD_HEADS = 16
SSD_HEAD_DIM = 64
SSD_INNER = 1024
SSD_GROUPS = 4
SSD_STATE = 128
SSD_CONV = 4
SSD_CHUNK = 128
SSD_XBC = SSD_INNER + 2 * SSD_GROUPS * SSD_STATE
DT_PAD = 128

MOBA_HEADS = 8
MOBA_HEAD_DIM = 128
MOBA_INNER = 1024
MOBA_BLOCK = 256
MOBA_TOPK = 3
REL_BUCKETS = 32
REL_MAX_DIST = 1024
MOBA_NEAR = 5
MOBA_QBLOCKS = 2
MOBA_CHAINS = 2
LOG2E = math.log2(math.e)

RWKV_HEADS = 16
RWKV_HEAD_DIM = 64
RWKV_INNER = 1024
RWKV_DECAY_LORA = 64
RWKV_ICLR_LORA = 64
RWKV_GATE_LORA = 128
RWKV_IN = 3 * RWKV_INNER + RWKV_DECAY_LORA + RWKV_ICLR_LORA + RWKV_GATE_LORA
RWKV_LN_EPS = 64e-5
RWKV_CHUNK = 64
RWKV_GROUP = 4

PEER_HEADS = 8
PEER_KEYS = 128
PEER_TOPK = 16
PEER_KEY_DIM = 256

IN_SSD = SSD_INNER + SSD_XBC + SSD_HEADS
IN_MOBA = 3 * MOBA_INNER


def _dot(a, b, precision=None):
    return jnp.dot(a, b, preferred_element_type=F32, precision=precision)


def _dot_nt(a, b, precision=None):
    return lax.dot_general(a, b, (((1,), (1,)), ((), ())), preferred_element_type=F32, precision=precision)


def _dot_tn(a, b, precision=None):
    return lax.dot_general(a, b, (((0,), (0,)), ((), ())), preferred_element_type=F32, precision=precision)


def _rms(x, g):
    return x * lax.rsqrt(jnp.mean(x * x, axis=-1, keepdims=True) + NORM_EPS) * g


def _silu(x):
    return x * jax.nn.sigmoid(x)


def _params(*sem):
    return pltpu.CompilerParams(dimension_semantics=sem, vmem_limit_bytes=VMEM_LIMIT)


def _norm_matmul_kernel(x_ref, g_ref, w_ref, o_ref, *, col_chunk):
    hb = _rms(x_ref[...], g_ref[...]).astype(BF16)
    for c in range(0, o_ref.shape[1], col_chunk):
        o_ref[:, c:c + col_chunk] = _dot(hb, w_ref[:, c:c + col_chunk])


def norm_matmul(x, g, w, tm=512, col_chunk=512):
    t, d = x.shape
    n = w.shape[1]
    col_chunk = math.gcd(n, col_chunk)
    return pl.pallas_call(
        functools.partial(_norm_matmul_kernel, col_chunk=col_chunk),
        grid=(t // tm,),
        in_specs=[pl.BlockSpec((tm, d), lambda i: (i, 0)),
                  pl.BlockSpec((1, d), lambda i: (0, 0)),
                  pl.BlockSpec((d, n), lambda i: (0, 0))],
        out_specs=pl.BlockSpec((tm, n), lambda i: (i, 0)),
        out_shape=jax.ShapeDtypeStruct((t, n), F32),
        compiler_params=_params("parallel"),
        name="norm_matmul",
    )(x, g.reshape(1, d), w)


def _merge_kernel(x_ref, ya_ref, yb_ref, yc_ref, g_ref, wg_ref, bg_ref, pa_ref, pb_ref, pc_ref, wo_ref, o_ref):
    x = x_ref[...]
    hb = _rms(x, g_ref[...]).astype(BF16)
    d = x.shape[1]
    merged = None
    for i, (y_ref, p_ref) in enumerate(((ya_ref, pa_ref), (yb_ref, pb_ref), (yc_ref, pc_ref))):
        gate = jax.nn.sigmoid(_dot(hb, wg_ref[:, i * d:(i + 1) * d]) + bg_ref[:, i * d:(i + 1) * d])
        term = gate * _dot(y_ref[...], p_ref[...])
        merged = term if merged is None else merged + term
    o_ref[...] = x + _dot(merged.astype(BF16), wo_ref[...])


def merge(x, ya, yb, yc, g, w_gate, b_gate, pa, pb, pc, w_out, tm=256):
    t, d = x.shape
    row = lambda i: (i, 0)
    fixed = lambda i: (0, 0)
    return pl.pallas_call(
        _merge_kernel,
        grid=(t // tm,),
        in_specs=[pl.BlockSpec((tm, d), row), pl.BlockSpec((tm, d), row), pl.BlockSpec((tm, d), row),
                  pl.BlockSpec((tm, d), row), pl.BlockSpec((1, d), fixed),
                  pl.BlockSpec((d, 3 * d), fixed), pl.BlockSpec((1, 3 * d), fixed),
                  pl.BlockSpec((d, d), fixed), pl.BlockSpec((d, d), fixed), pl.BlockSpec((d, d), fixed),
                  pl.BlockSpec((d, d), fixed)],
        out_specs=pl.BlockSpec((tm, d), row),
        out_shape=jax.ShapeDtypeStruct((t, d), F32),
        compiler_params=_params("parallel"),
        name="merge",
    )(x, ya, yb, yc, g.reshape(1, d), w_gate, b_gate.reshape(1, 3 * d), pa, pb, pc, w_out)


def _ssd_kernel(xbc_ref, z_ref, dt_ref, cw_ref, cb_ref, dtb_ref, a_ref, dsk_ref, ng_ref, rexp_ref, tri_ref,
                o_ref, buf_ref, state_ref):
    lc = SSD_CHUNK
    c = pl.program_id(1)

    @pl.when(c == 0)
    def _():
        buf_ref[0:8, :] = jnp.zeros((8, SSD_XBC), F32)
        state_ref[...] = jnp.zeros_like(state_ref)

    cur = xbc_ref[...]
    buf_ref[8:8 + lc, :] = cur
    acc = cb_ref[...] + cw_ref[0:1, :] * buf_ref[5:5 + lc, :]
    for k in range(1, SSD_CONV):
        acc = acc + cw_ref[k:k + 1, :] * buf_ref[5 + k:5 + k + lc, :]
    buf_ref[0:8, :] = cur[lc - 8:lc, :]
    xbc = _silu(acc)
    xs = xbc[:, :SSD_INNER]

    dt = jax.nn.softplus(dt_ref[...] + dtb_ref[...])
    acs = _dot(tri_ref[...], dt * a_ref[...], HIGHEST)
    acs_t = acs.T
    rexp = rexp_ref[...]
    dt_x = _dot(dt, rexp, HIGHEST)
    e_x = _dot(jnp.exp(acs), rexp, HIGHEST)
    dte_x = _dot(jnp.exp(acs[lc - 1:lc, :] - acs), rexp, HIGHEST)
    xdt = xs * dt_x
    xdt_b = xdt.astype(BF16)
    xw_b = (xdt * dte_x).astype(BF16)
    chunk_decay = e_x[lc - 1:lc, :]

    row = lax.broadcasted_iota(jnp.int32, (lc, lc), 0)
    col = lax.broadcasted_iota(jnp.int32, (lc, lc), 1)
    causal = row >= col
    gw = SSD_INNER // SSD_GROUPS
    hpg = SSD_HEADS // SSD_GROUPS
    n = SSD_STATE
    ys = []
    for g in range(SSD_GROUPS):
        b_g = xbc[:, SSD_INNER + g * n:SSD_INNER + (g + 1) * n].astype(BF16)
        c_g = xbc[:, SSD_INNER + (SSD_GROUPS + g) * n:SSD_INNER + (SSD_GROUPS + g + 1) * n].astype(BF16)
        cb = _dot_nt(c_g, b_g)
        s_in = state_ref[:, g * gw:(g + 1) * gw]
        y_off = _dot(c_g, s_in.astype(BF16)) * e_x[:, g * gw:(g + 1) * gw]
        state_ref[:, g * gw:(g + 1) * gw] = (chunk_decay[:, g * gw:(g + 1) * gw] * s_in
                                             + _dot_tn(b_g, xw_b[:, g * gw:(g + 1) * gw]))
        diag = []
        for e in range(hpg):
            h = g * hpg + e
            seg = jnp.exp(jnp.where(causal, acs[:, h:h + 1] - acs_t[h:h + 1, :], NEG))
            m = (cb * seg).astype(BF16)
            diag.append(_dot(m, xdt_b[:, h * SSD_HEAD_DIM:(h + 1) * SSD_HEAD_DIM]))
        ys.append(jnp.concatenate(diag, axis=-1) + y_off)
    y = jnp.concatenate(ys, axis=-1) + xs * dsk_ref[...]
    y = y * _silu(z_ref[...])
    for g in range(SSD_GROUPS):
        yg = y[:, g * gw:(g + 1) * gw]
        yg = yg * lax.rsqrt(jnp.mean(yg * yg, axis=-1, keepdims=True) + NORM_EPS)
        o_ref[:, g * gw:(g + 1) * gw] = (yg * ng_ref[:, g * gw:(g + 1) * gw]).astype(o_ref.dtype)


def ssd(proj, bsz, slen, conv_w, conv_b, dt_bias, a_log, d_skip, norm_g):
    lc = SSD_CHUNK
    nc = slen // lc
    t = bsz * slen
    pad = DT_PAD - SSD_HEADS
    dtb = jnp.pad(dt_bias.astype(F32), (0, pad)).reshape(1, DT_PAD)
    a = jnp.pad(-jnp.exp(a_log.astype(F32)), (0, pad)).reshape(1, DT_PAD)
    dsk = jnp.repeat(d_skip.astype(F32), SSD_HEAD_DIM).reshape(1, SSD_INNER)
    rexp = jnp.asarray((np.arange(DT_PAD)[:, None] == np.arange(SSD_INNER)[None, :] // SSD_HEAD_DIM)
                       .astype(np.float32))
    tri = jnp.asarray(np.tril(np.ones((lc, lc), np.float32)))
    rowmap = lambda b, c: (b * nc + c, 0)
    fixed = lambda b, c: (0, 0)
    zblk = SSD_XBC // SSD_INNER
    dtblk = (SSD_XBC + SSD_INNER) // DT_PAD
    return pl.pallas_call(
        _ssd_kernel,
        grid=(bsz, nc),
        in_specs=[pl.BlockSpec((lc, SSD_XBC), rowmap),
                  pl.BlockSpec((lc, SSD_INNER), lambda b, c: (b * nc + c, zblk)),
                  pl.BlockSpec((lc, DT_PAD), lambda b, c: (b * nc + c, dtblk)),
                  pl.BlockSpec((SSD_CONV, SSD_XBC), fixed), pl.BlockSpec((1, SSD_XBC), fixed),
                  pl.BlockSpec((1, DT_PAD), fixed), pl.BlockSpec((1, DT_PAD), fixed),
                  pl.BlockSpec((1, SSD_INNER), fixed), pl.BlockSpec((1, SSD_INNER), fixed),
                  pl.BlockSpec((DT_PAD, SSD_INNER), fixed), pl.BlockSpec((lc, lc), fixed)],
        out_specs=pl.BlockSpec((lc, SSD_INNER), rowmap),
        out_shape=jax.ShapeDtypeStruct((t, SSD_INNER), BF16),
        scratch_shapes=[pltpu.VMEM((lc + 8, SSD_XBC), F32), pltpu.VMEM((SSD_STATE, SSD_INNER), F32)],
        compiler_params=_params("parallel", "arbitrary"),
        name="ssd",
    )(proj, proj, proj, conv_w.astype(F32), conv_b.astype(F32).reshape(1, SSD_XBC), dtb, a, dsk,
      norm_g.astype(F32).reshape(1, SSD_INNER), rexp, tri)


def _ssd_weight(w_in_l):
    w = w_in_l[:, :IN_SSD]
    z, xbc, dt = w[:, :SSD_INNER], w[:, SSD_INNER:SSD_INNER + SSD_XBC], w[:, SSD_INNER + SSD_XBC:]
    dt = jnp.pad(dt, ((0, 0), (0, DT_PAD - SSD_HEADS)))
    return jnp.concatenate([xbc, z, dt], axis=1).astype(BF16)


def _t5_bucket(dist):
    max_exact = REL_BUCKETS // 2
    d = jnp.maximum(dist, 0)
    df = jnp.maximum(d, 1).astype(F32)
    large = max_exact + (jnp.log(df / max_exact) / math.log(REL_MAX_DIST / max_exact)
                         * (REL_BUCKETS - max_exact)).astype(jnp.int32)
    large = jnp.minimum(large, REL_BUCKETS - 1)
    return jnp.where(d < max_exact, d, large)


def _bias_table_kernel(bias_ref, bucket_ref, o_ref):
    h = pl.program_id(0)
    for d in range(MOBA_NEAR):
        bk = bucket_ref[d]
        acc = jnp.where(bk < 0, NEG, 0.0)
        for b in range(REL_BUCKETS):
            acc = jnp.where(bk == b, bias_ref[h, b] * LOG2E, acc)
        o_ref[d] = acc
    o_ref[MOBA_NEAR] = jnp.zeros(o_ref.shape[1:], F32)


def moba_bias_table(rel_bias):
    blk = MOBA_BLOCK
    q = jnp.arange(blk, dtype=jnp.int32)[None, :, None]
    k = jnp.arange(blk, dtype=jnp.int32)[None, None, :]
    dist = jnp.arange(MOBA_NEAR, dtype=jnp.int32)[:, None, None] * blk + q - k
    bucket = jnp.where(dist < 0, -1, _t5_bucket(dist))
    return pl.pallas_call(
        _bias_table_kernel,
        grid=(MOBA_HEADS,),
        in_specs=[pl.BlockSpec(memory_space=pltpu.SMEM),
                  pl.BlockSpec((MOBA_NEAR, blk, blk), lambda h: (0, 0, 0))],
        out_specs=pl.BlockSpec((None, MOBA_NEAR + 1, blk, blk), lambda h: (h, 0, 0, 0)),
        out_shape=jax.ShapeDtypeStruct((MOBA_HEADS, MOBA_NEAR + 1, blk, blk), F32),
        compiler_params=_params("arbitrary"),
        name="moba_bias_table",
    )(rel_bias.astype(F32).T, bucket)


def _moba_prep_kernel(q_ref, k_ref, v_ref, qg_ref, kg_ref, far_ref, qa_ref, ka_ref, va_ref, kmean_ref):
    i = pl.program_id(2)
    blk, dh = q_ref.shape
    half = dh // 2

    @pl.when(i == 0)
    def _():
        kmean_ref[...] = jnp.zeros_like(kmean_ref)

    qn = _rms(q_ref[...], qg_ref[...])
    kn = _rms(k_ref[...], kg_ref[...])
    gate = _dot_nt(qn, kmean_ref[...], HIGHEST)
    col = lax.broadcasted_iota(jnp.int32, (blk, dh), 1)
    colf = col.astype(F32)
    valid = col < i
    g = jnp.where(valid, gate, NEG)
    sel = col < 0
    for _ in range(MOBA_TOPK):
        m = jnp.max(g, axis=-1, keepdims=True)
        idx = jnp.min(jnp.where(g == m, colf, float(dh)), axis=-1, keepdims=True)
        hit = colf == idx
        sel = sel | hit
        g = jnp.where(hit, -jnp.inf, g)
    sel = sel & valid

    far = far_ref[...] * LOG2E
    far_hi = far.astype(BF16).astype(F32)
    far_lo = far - far_hi
    sel_f = jnp.where(sel, 1.0, 0.0)
    low = col >= half
    sel2 = jnp.where(low, pltpu.roll(sel_f, half, 1), sel_f) > 0.5
    jcol = jnp.where(low, col - half, col)
    is_far = (i - jcol) >= MOBA_NEAR
    pen_hi = jnp.where(jcol == i, 0.0, jnp.where(sel2, jnp.where(is_far, far_hi, 0.0), NEG))
    pen_lo = jnp.where(sel2 & is_far, far_lo, 0.0)
    qa_ref[:, 0:dh] = (qn * (dh ** -0.5 * LOG2E)).astype(BF16)
    qa_ref[:, dh:2 * dh] = jnp.where(low, pen_lo, pen_hi).astype(BF16)
    ka_ref[:, 0:dh] = kn.astype(BF16)
    ka_ref[:, dh:2 * dh] = jnp.where(jcol == i, 1.0, 0.0).astype(BF16)
    va_ref[:, 0:dh] = v_ref[...].astype(BF16)
    va_ref[:, dh:2 * dh] = jnp.ones((blk, dh), BF16)
    kmean_ref[pl.ds(i, 1), :] = jnp.mean(kn, axis=0, keepdims=True)


def _moba_attn_kernel(qa_ref, ka_ref, va_ref, tab_ref, o_ref, m_ref, acc_ref):
    a = pl.program_id(2)
    blk = MOBA_BLOCK
    dh = m_ref.shape[-1]
    chains = range(qa_ref.shape[0])
    q = [qa_ref[c] for c in chains]
    m_ref[...] = jnp.full_like(m_ref, 0.1 * NEG)
    acc_ref[...] = jnp.zeros_like(acc_ref)

    def step(off, nk, tab):
        s = [_dot_nt(q[c], ka_ref[c, pl.ds(off, nk), :]) for c in chains]
        if tab is not None:
            s = [x + tab for x in s]
        m_old = [m_ref[c] for c in chains]
        m_new = [jnp.maximum(mo, jnp.max(x, axis=-1, keepdims=True)) for mo, x in zip(m_old, s)]
        alpha = [jnp.exp2(mo - mn) for mo, mn in zip(m_old, m_new)]
        p = [jnp.exp2(x - jnp.concatenate([mn] * (nk // dh), axis=-1)).astype(BF16) for x, mn in zip(s, m_new)]
        pv = [_dot(p[c], va_ref[c, pl.ds(off, nk), :]) for c in chains]
        for c in chains:
            acc_ref[c] = jnp.concatenate([alpha[c], alpha[c]], axis=-1) * acc_ref[c] + pv[c]
            m_ref[c] = m_new[c]

    first = MOBA_QBLOCKS * a
    n_far = jnp.maximum(first - (MOBA_NEAR - 1), 0)

    def far_body(t, c):
        step(pl.multiple_of(t * (MOBA_QBLOCKS * blk), MOBA_QBLOCKS * blk), MOBA_QBLOCKS * blk, None)
        return c

    def near_body(j, c):
        tab = jnp.concatenate([tab_ref[jnp.clip(first + r - j, 0, MOBA_NEAR)] for r in range(MOBA_QBLOCKS)], axis=0)
        step(pl.multiple_of(j * blk, blk), blk, tab)
        return c

    lax.fori_loop(0, n_far // MOBA_QBLOCKS, far_body, 0)
    lax.fori_loop(n_far, first + MOBA_QBLOCKS, near_body, 0)
    for c in chains:
        acc = acc_ref[c]
        o_ref[c] = (acc[:, :dh] / acc[:, dh:]).astype(o_ref.dtype)


def moba(proj, bsz, slen, q_norm_g, k_norm_g, bias_table, far_bias):
    blk, dh, nh = MOBA_BLOCK, MOBA_HEAD_DIM, MOBA_HEADS
    nb = slen // blk
    tq = MOBA_QBLOCKS * blk
    assert slen % tq == 0 and nb <= dh // 2 and (MOBA_NEAR - 1) % MOBA_QBLOCKS == 0
    t = bsz * slen
    far = jnp.broadcast_to(far_bias.astype(F32)[:, None, None], (nh, 1, dh))
    aug = pl.BlockSpec((None, None, blk, 2 * dh), lambda b, h, i: (b, h, i, 0))
    qa, ka, va = pl.pallas_call(
        _moba_prep_kernel,
        grid=(bsz, nh, nb),
        in_specs=[pl.BlockSpec((blk, dh), lambda b, h, i: (b * nb + i, h)),
                  pl.BlockSpec((blk, dh), lambda b, h, i: (b * nb + i, nh + h)),
                  pl.BlockSpec((blk, dh), lambda b, h, i: (b * nb + i, 2 * nh + h)),
                  pl.BlockSpec((1, dh), lambda b, h, i: (0, 0)),
                  pl.BlockSpec((1, dh), lambda b, h, i: (0, 0)),
                  pl.BlockSpec((None, 1, dh), lambda b, h, i: (h, 0, 0))],
        out_specs=[aug, aug, aug],
        out_shape=[jax.ShapeDtypeStruct((bsz, nh, slen, 2 * dh), BF16)] * 3,
        scratch_shapes=[pltpu.VMEM((dh, dh), F32)],
        compiler_params=_params("parallel", "parallel", "arbitrary"),
        name="moba_prep",
    )(proj, proj, proj, q_norm_g.astype(F32).reshape(1, dh), k_norm_g.astype(F32).reshape(1, dh), far)
    nch = MOBA_CHAINS
    assert bsz % nch == 0
    whole = pl.BlockSpec((nch, None, slen, 2 * dh), lambda b, h, a: (b, h, 0, 0), pipeline_mode=pl.Buffered(1))
    out = pl.pallas_call(
        _moba_attn_kernel,
        grid=(bsz // nch, nh, slen // tq),
        in_specs=[pl.BlockSpec((nch, None, tq, 2 * dh), lambda b, h, a: (b, h, a, 0)), whole, whole,
                  pl.BlockSpec((None, MOBA_NEAR + 1, blk, blk), lambda b, h, a: (h, 0, 0, 0))],
        out_specs=pl.BlockSpec((nch, tq, dh), lambda b, h, a: (b, a, h)),
        out_shape=jax.ShapeDtypeStruct((bsz, slen, nh * dh), BF16),
        scratch_shapes=[pltpu.VMEM((nch, tq, dh), F32), pltpu.VMEM((nch, tq, 2 * dh), F32)],
        compiler_params=_params("parallel", "parallel", "arbitrary"),
        name="moba_attn",
    )(qa, ka, va, bias_table)
    return out.reshape(t, nh * dh)


def _rwkv_prep_kernel(cur_ref, prev_ref, mu_ref, w0_ref, ww2_ref, a0_ref, wa2_ref, wg2_ref, kk_ref, ka_ref,
                      hsum_ref, hexp_ref, r_ref, lw_ref, k_ref, v_ref, a_ref, b_ref, g_ref, buf_ref,
                      *, tiles_per_seq):
    i = pl.program_id(0)
    tm = cur_ref.shape[0]
    c = RWKV_INNER
    cur = cur_ref[...]
    first = (i % tiles_per_seq) == 0
    buf_ref[8:8 + tm, :] = cur
    buf_ref[0:8, :] = jnp.where(first, 0.0, prev_ref[...])
    prev = buf_ref[7:7 + tm, :]
    xs = cur + (prev - cur) * mu_ref[...]
    r, k, v = xs[:, :c], xs[:, c:2 * c], xs[:, 2 * c:3 * c]
    o1 = 3 * c + RWKV_DECAY_LORA
    o2 = o1 + RWKV_ICLR_LORA
    wl, al, gl = xs[:, 3 * c:o1], xs[:, o1:o2], xs[:, o2:]
    w = -jax.nn.softplus(-(w0_ref[...] + _dot(jnp.tanh(wl), ww2_ref[...], HIGHEST))) - 0.5
    a = jax.nn.sigmoid(a0_ref[...] + _dot(al, wa2_ref[...], HIGHEST))
    kk = k * kk_ref[...]
    ss = _dot(_dot(kk * kk, hsum_ref[...], HIGHEST), hexp_ref[...], HIGHEST)
    kk = kk * lax.rsqrt(jnp.maximum(ss, 1e-12))
    r_ref[...] = r
    lw_ref[...] = -jnp.exp(w)
    k_ref[...] = k * (1.0 + (a - 1.0) * ka_ref[...])
    v_ref[...] = v
    a_ref[...] = -kk
    b_ref[...] = kk * a
    g_ref[...] = _dot(jax.nn.sigmoid(gl), wg2_ref[...], HIGHEST)


def _bd_rows(y, mask):
    return jnp.where(mask, jnp.concatenate([y] * RWKV_GROUP, axis=0), 0.0)


def _rwkv_scan_kernel(r_ref, lw_ref, k_ref, v_ref, a_ref, b_ref, g_ref, rk_ref, lng_ref, lnb_ref,
                      tri_ref, lows_ref, lowi_ref, eye_ref, bd_ref, avg_ref, o_ref, s_ref, *, mxu_dtype):
    ci = pl.program_id(2)
    ch = RWKV_CHUNK

    @pl.when(ci == 0)
    def _():
        s_ref[...] = jnp.zeros_like(s_ref)

    prec = HIGHEST if mxu_dtype == F32 else None

    def mm(x, y):
        return _dot(x.astype(mxu_dtype), y.astype(mxu_dtype), prec)

    def mm_nt(x, y):
        return _dot_nt(x.astype(mxu_dtype), y.astype(mxu_dtype), prec)

    def mm_tn(x, y):
        return _dot_tn(x.astype(mxu_dtype), y.astype(mxu_dtype), prec)

    bd = bd_ref[...] > 0.5
    strict = lows_ref[...] > 0.5
    incl = lowi_ref[...] > 0.5
    gw = bd_ref.shape[0]
    eye_big = lax.broadcasted_iota(jnp.int32, (gw, gw), 0) == lax.broadcasted_iota(jnp.int32, (gw, gw), 1)
    avg = avg_ref[...]

    ng = s_ref.shape[0]
    cols = [slice(gi * gw, (gi + 1) * gw) for gi in range(ng)]

    def each(fn, *lists):
        return [fn(*xs) for xs in zip(*lists)]

    def bdr(ys):
        return [_bd_rows(y, bd) for y in ys]

    r, lw, k, v, a, b = ([ref[:, c] for c in cols] for ref in (r_ref, lw_ref, k_ref, v_ref, a_ref, b_ref))
    cl = each(lambda x: _dot(tri_ref[...], x, HIGHEST), lw)
    cl_end = [x[ch - 1:ch, :] for x in cl]
    e_neg = [jnp.exp(-x) for x in cl]
    a_t = each(lambda x, c, l: x * jnp.exp(c - l), a, cl, lw)
    r_t = each(lambda x, c: x * jnp.exp(c), r, cl)
    b_t = each(jnp.multiply, b, e_neg)
    k_t = each(jnp.multiply, k, e_neg)
    e_end = each(lambda ce, c: jnp.exp(ce - c), cl_end, cl)
    b_h = each(jnp.multiply, b, e_end)
    k_h = each(jnp.multiply, k, e_end)

    ar = each(lambda x, y: jnp.concatenate([x, y], axis=0), a_t, r_t)
    gb = each(mm_nt, ar, bdr(b_t))
    gk = each(mm_nt, ar, bdr(k_t))
    a_ab = [jnp.where(strict, x[:ch], 0.0) for x in gb]
    a_ak = [jnp.where(strict, x[:ch], 0.0) for x in gk]
    q_rb = [jnp.where(incl, x[ch:], 0.0) for x in gb]
    q_rk = [jnp.where(incl, x[ch:], 0.0) for x in gk]

    t_inv = [eye_ref[...] + x for x in a_ab]
    p = a_ab
    for _ in range(int(math.log2(ch)) - 1):
        p = each(mm, p, bdr(p))
        t_inv = each(lambda t, tp: t + tp, t_inv, each(mm, t_inv, bdr(p)))

    vbd = bdr(v)
    w_t = each(mm, t_inv, bdr(a_t))
    u0 = each(mm, t_inv, bdr(each(mm, a_ak, vbd)))
    z = each(lambda x, y: x + y, r_t, each(mm, q_rb, bdr(w_t)))
    y_loc = each(lambda x, y: x + y, each(mm, q_rb, bdr(u0)), each(mm, q_rk, vbd))
    bwu = each(lambda x, w, u: mm_tn(x, jnp.concatenate([w, u], axis=1)), b_h, w_t, u0)
    kv = each(mm_tn, k_h, v)
    m_s = each(lambda x, ce: jnp.where(bd, x[:, :gw], 0.0) + jnp.where(eye_big, jnp.exp(ce), 0.0), bwu, cl_end)
    n_s = each(lambda x, y: jnp.where(bd, x[:, gw:] + y, 0.0), bwu, kv)
    s0 = [s_ref[gi] for gi in range(ng)]
    zs = each(lambda x, m, s: mm(jnp.concatenate([x, m], axis=0), s), z, m_s, s0)
    y = each(lambda x, yl: x[:ch] + yl, zs, y_loc)
    for gi in range(ng):
        s_ref[gi] = zs[gi][ch:] + n_s[gi]

    mean = [_dot(x, avg, HIGHEST) for x in y]
    dlt = each(lambda x, m: x - m, y, mean)
    var = [_dot(x * x, avg, HIGHEST) for x in dlt]
    rkk = each(lambda x, kk, c: _dot(x * kk * rk_ref[:, c], avg, HIGHEST), r, k, cols)
    for gi, c in enumerate(cols):
        yn = dlt[gi] * lax.rsqrt(var[gi] + RWKV_LN_EPS) * lng_ref[:, c] + lnb_ref[:, c]
        bonus = rkk[gi] * float(RWKV_HEAD_DIM) * v[gi]
        o_ref[:, c] = ((yn + bonus) * g_ref[:, c]).astype(o_ref.dtype)


def rwkv(proj, bsz, slen, mu, w0, w_w2, a0, w_a2, w_g2, k_k, k_a, r_k, lnx_g, lnx_b, tm=256, mxu_dtype=BF16,
         groups_per_step=4):
    c, ch, grp, n = RWKV_INNER, RWKV_CHUNK, RWKV_GROUP, RWKV_HEAD_DIM
    t = bsz * slen
    f = lambda p: p.astype(F32).reshape(1, -1)
    heads = np.arange(c) // n
    hsum = jnp.asarray((heads[:, None] == np.arange(128)[None, :]).astype(np.float32))
    hexp = hsum.T
    row = lambda i: (i, 0)
    fixed = lambda i: (0, 0)
    wide = pl.BlockSpec((tm, c), row)
    vec = pl.BlockSpec((1, c), fixed)
    r, lw, k, v, a, b, g = pl.pallas_call(
        functools.partial(_rwkv_prep_kernel, tiles_per_seq=slen // tm),
        grid=(t // tm,),
        in_specs=[pl.BlockSpec((tm, RWKV_IN), row),
                  pl.BlockSpec((8, RWKV_IN), lambda i: (jnp.maximum(i * (tm // 8) - 1, 0), 0)),
                  pl.BlockSpec((1, RWKV_IN), fixed), vec,
                  pl.BlockSpec((RWKV_DECAY_LORA, c), fixed), vec,
                  pl.BlockSpec((RWKV_ICLR_LORA, c), fixed), pl.BlockSpec((RWKV_GATE_LORA, c), fixed),
                  vec, vec, pl.BlockSpec((c, 128), fixed), pl.BlockSpec((128, c), fixed)],
        out_specs=[wide] * 7,
        out_shape=[jax.ShapeDtypeStruct((t, c), F32)] * 7,
        scratch_shapes=[pltpu.VMEM((tm + 8, RWKV_IN), F32)],
        compiler_params=_params("parallel"),
        name="rwkv_prep",
    )(proj, proj, f(mu), f(w0), w_w2.astype(F32), f(a0), w_a2.astype(F32), w_g2.astype(F32), f(k_k), f(k_a),
      hsum, hexp)

    gw = grp * n
    nc = slen // ch
    tt = np.arange(ch)
    colt = np.arange(gw) % ch
    hrow = np.arange(gw) // n
    tri = jnp.asarray((tt[:, None] >= tt[None, :]).astype(np.float32))
    lows = jnp.asarray((colt[None, :] < tt[:, None]).astype(np.float32))
    lowi = jnp.asarray((colt[None, :] <= tt[:, None]).astype(np.float32))
    eye = jnp.asarray((colt[None, :] == tt[:, None]).astype(np.float32))
    bdm = (hrow[:, None] == hrow[None, :]).astype(np.float32)
    bd = jnp.asarray(bdm)
    avg = jnp.asarray(bdm / n)
    sw = groups_per_step * gw
    blk = pl.BlockSpec((ch, sw), lambda bi, gi, ci: (bi * nc + ci, gi))
    gvec = pl.BlockSpec((1, sw), lambda bi, gi, ci: (0, gi))
    const = lambda shape: pl.BlockSpec(shape, lambda bi, gi, ci: (0, 0))
    return pl.pallas_call(
        functools.partial(_rwkv_scan_kernel, mxu_dtype=mxu_dtype),
        grid=(bsz, c // sw, nc),
        in_specs=[blk] * 7 + [gvec] * 3 + [const((ch, ch)), const((ch, gw)), const((ch, gw)), const((ch, gw)),
                                           const((gw, gw)), const((gw, gw))],
        out_specs=blk,
        out_shape=jax.ShapeDtypeStruct((t, c), BF16),
        scratch_shapes=[pltpu.VMEM((groups_per_step, gw, gw), F32)],
        compiler_params=_params("parallel", "parallel", "arbitrary"),
        name="rwkv_scan",
    )(r, lw, k, v, a, b, g, f(r_k), f(lnx_g), f(lnx_b), tri, lows, lowi, eye, bd, avg)


PEER_RANKS = PEER_TOPK + 1
PEER_PAIRS = [(p, q) for p in range(PEER_RANKS) for q in range(PEER_RANKS // (p + 1))]
PEER_CAND_ROWS = -(-len(PEER_PAIRS) // 8) * 8
PEER_TOP_ROWS = -(-2 * PEER_RANKS // 8) * 8


def _peer_kernel(x_ref, g_ref, wq_ref, k1_ref, k2_ref, u_ref, vt_ref, o_ref,
                 h2_ref, thr_ref, e1_ref, e2_ref, acc_ref, top_ref, cand_ref, hu0_ref, hu1_ref, p0_ref, p1_ref):
    e = pl.program_id(1)
    n_blocks = 2 * (pl.num_programs(1) - 1)
    nk = PEER_KEYS
    half = PEER_KEY_DIM // 2
    eb = u_ref.shape[0] // 2

    @pl.when(e == 0)
    def _():
        for ref in (hu0_ref, hu1_ref, p0_ref, p1_ref):
            ref[...] = jnp.zeros_like(ref)
        hb = _rms(x_ref[...], g_ref[...]).astype(BF16)
        h2_ref[...] = hb
        q = _dot(hb, wq_ref[...])
        cand_ref[...] = jnp.full_like(cand_ref, -jnp.inf)
        for h in range(PEER_HEADS):
            base = h * PEER_KEY_DIM
            s1 = _dot_nt(k1_ref[...], q[:, base:base + half], HIGHEST)
            s2 = _dot_nt(k2_ref[...], q[:, base + half:base + 2 * half], HIGHEST)
            for side, s in enumerate((s1, s2)):
                cur = s
                for rnk in range(PEER_RANKS):
                    m = jnp.max(cur, axis=0, keepdims=True)
                    top_ref[side * PEER_RANKS + rnk:side * PEER_RANKS + rnk + 1, :] = m
                    cur = jnp.where(cur == m, -jnp.inf, cur)
            for ci, (p, qq) in enumerate(PEER_PAIRS):
                cand_ref[ci:ci + 1, :] = top_ref[p:p + 1, :] + top_ref[PEER_RANKS + qq:PEER_RANKS + qq + 1, :]
            cand = cand_ref[...]
            cur = cand
            best = jnp.max(cur, axis=0, keepdims=True)
            tau = best
            for _ in range(PEER_TOPK - 1):
                cur = jnp.where(cur == tau, -jnp.inf, cur)
                tau = jnp.max(cur, axis=0, keepdims=True)
            nxt = jnp.max(jnp.where(cur == tau, -jnp.inf, cur), axis=0, keepdims=True)
            zsum = jnp.sum(jnp.where(cand >= tau, jnp.exp(cand - best), 0.0), axis=0, keepdims=True)
            b0 = top_ref[PEER_RANKS:PEER_RANKS + 1, :]
            thr_ref[h] = jnp.exp(0.5 * (tau + nxt) - s1 - b0)
            e1_ref[h] = jnp.exp(s1 - top_ref[0:1, :]) / zsum
            e2_ref[h] = jnp.exp(s2 - b0)
        acc_ref[...] = jnp.zeros_like(acc_ref)

    hu_slots, p_slots = (hu0_ref, hu1_ref), (p0_ref, p1_ref)
    tt = h2_ref.shape[0]
    d = acc_ref.shape[0]
    tok = [slice(n * (tt // 2), (n + 1) * (tt // 2)) for n in range(2)]
    for par in range(2):
        tick = 2 * e + par
        blk_b = jnp.clip(tick - 1, 0, n_blocks - 1)
        p_rd, p_wr, hu_rd, hu_wr = p_slots[par], p_slots[1 - par], hu_slots[1 - par], hu_slots[par]

        def down(n, r):
            rows = slice(r * (d // 2), (r + 1) * (d // 2))
            acc_ref[rows, tok[n]] += _dot(vt_ref[rows, par * eb:(par + 1) * eb], p_rd[:, tok[n]])

        def up(n, r):
            rows = slice(r * (eb // 2), (r + 1) * (eb // 2))
            hu_wr[rows, tok[n]] = _dot_nt(u_ref[par * eb + r * (eb // 2):par * eb + (r + 1) * (eb // 2), :],
                                          h2_ref[tok[n], :])

        def weigh(n, ib):
            rows = slice(ib * nk, (ib + 1) * nk)
            i = blk_b * (eb // nk) + ib
            hu = hu_rd[rows, tok[n]]
            act = 0.5 * hu * (1.0 + lax.erf(hu * (2.0 ** -0.5)))
            w = None
            for h in range(PEER_HEADS):
                e2 = e2_ref[h, :, tok[n]]
                term = jnp.where(e2 >= thr_ref[h, pl.ds(i, 1), tok[n]], e2 * e1_ref[h, pl.ds(i, 1), tok[n]], 0.0)
                w = term if w is None else w + term
            p_wr[rows, tok[n]] = (w * act).astype(BF16)

        matmul_jobs = [(fn, n, r) for n in range(2) for r in range(2) for fn in (down, up)]
        vector_jobs = [(n, ib) for n in range(2) for ib in range(eb // nk)]
        assert len(matmul_jobs) == len(vector_jobs)
        for (fn, n, r), (vn, ib) in zip(matmul_jobs, vector_jobs):
            fn(n, r)
            weigh(vn, ib)

    @pl.when(e == pl.num_programs(1) - 1)
    def _():
        o_ref[...] = x_ref[...] + acc_ref[...].T


def peer(x, g, wq, k1, k2, u, vt, tt=512, eb=512):
    t, d = x.shape
    ne = u.shape[0]
    nh, nk = PEER_HEADS, PEER_KEYS
    nb2 = ne // (2 * eb)
    assert ne % (2 * eb) == 0
    return pl.pallas_call(
        _peer_kernel,
        grid=(t // tt, nb2 + 1),
        in_specs=[pl.BlockSpec((tt, d), lambda i, e: (i, 0)),
                  pl.BlockSpec((1, d), lambda i, e: (0, 0)),
                  pl.BlockSpec((d, nh * PEER_KEY_DIM), lambda i, e: (0, 0)),
                  pl.BlockSpec((nk, PEER_KEY_DIM // 2), lambda i, e: (0, 0)),
                  pl.BlockSpec((nk, PEER_KEY_DIM // 2), lambda i, e: (0, 0)),
                  pl.BlockSpec((2 * eb, d), lambda i, e: (jnp.minimum(e, nb2 - 1), 0)),
                  pl.BlockSpec((d, 2 * eb), lambda i, e: (0, jnp.maximum(e - 1, 0)))],
        out_specs=pl.BlockSpec((tt, d), lambda i, e: (i, 0)),
        out_shape=jax.ShapeDtypeStruct((t, d), F32),
        scratch_shapes=[pltpu.VMEM((tt, d), BF16),
                        pltpu.VMEM((nh, nk, tt), F32), pltpu.VMEM((nh, nk, tt), F32),
                        pltpu.VMEM((nh, nk, tt), F32),
                        pltpu.VMEM((d, tt), F32),
                        pltpu.VMEM((PEER_TOP_ROWS, tt), F32), pltpu.VMEM((PEER_CAND_ROWS, tt), F32),
                        pltpu.VMEM((eb, tt), F32), pltpu.VMEM((eb, tt), F32),
                        pltpu.VMEM((eb, tt), BF16), pltpu.VMEM((eb, tt), BF16)],
        compiler_params=_params("parallel", "arbitrary"),
        name="peer",
    )(x, g.astype(F32).reshape(1, d), wq, k1.astype(F32), k2.astype(F32), u, vt)


def kernel(x, rel_bias, norm1_g, w_in, conv_w, conv_b, dt_bias, a_log, d_skip, ssd_norm_g, q_norm_g, k_norm_g, rwkv_mu, w0, w_w2, a0, w_a2, w_g2, k_k, k_a, r_k, lnx_g, lnx_b, p_ssd, p_moba, p_rwkv, w_gate, b_gate, w_out, norm2_g, peer_wq, peer_k1, peer_k2, peer_u, peer_v):
    bsz, slen, d = x.shape
    t = bsz * slen
    xf = x.astype(F32).reshape(t, d)
    bias_table = moba_bias_table(rel_bias)
    far_bias = rel_bias[REL_BUCKETS - 1]
    for l in range(w_in.shape[0]):
        g1 = norm1_g[l].astype(F32)
        w_moba = w_in[l][:, IN_SSD:IN_SSD + IN_MOBA].astype(BF16)
        w_rwkv = w_in[l][:, IN_SSD + IN_MOBA:].astype(BF16)
        ya = ssd(norm_matmul(xf, g1, _ssd_weight(w_in[l])), bsz, slen, conv_w[l], conv_b[l], dt_bias[l],
                 a_log[l], d_skip[l], ssd_norm_g[l])
        yb = moba(norm_matmul(xf, g1, w_moba), bsz, slen, q_norm_g[l], k_norm_g[l], bias_table, far_bias)
        yc = rwkv(norm_matmul(xf, g1, w_rwkv), bsz, slen, rwkv_mu[l], w0[l], w_w2[l], a0[l], w_a2[l], w_g2[l],
                  k_k[l], k_a[l], r_k[l], lnx_g[l], lnx_b[l])
        xf = merge(xf, ya, yb, yc, g1, w_gate[l].astype(BF16), b_gate[l].astype(F32), p_ssd[l].astype(BF16),
                   p_moba[l].astype(BF16), p_rwkv[l].astype(BF16), w_out[l].astype(BF16))
        xf = peer(xf, norm2_g[l], peer_wq[l].astype(BF16), peer_k1[l], peer_k2[l], peer_u[l].astype(BF16),
                  peer_v[l].astype(BF16).T)
    return xf.reshape(bsz, slen, d).astype(x.dtype)
```

```python
import functools
import math

import numpy as np
import jax
import jax.numpy as jnp
from jax import lax
from jax.experimental import pallas as pl
from jax.experimental.pallas import tpu as pltpu

F32 = jnp.float32
BF16 = jnp.bfloat16
HIGHEST = lax.Precision.HIGHEST

D_MODEL = 1024
NORM_EPS = 1e-6
NEG = -1e30
VMEM_LIMIT = 56 * 1024 * 1024

SSD_HEADS = 16
SSD_HEAD_DIM = 64
SSD_INNER = 1024
SSD_GROUPS = 4
SSD_STATE = 128
SSD_CONV = 4
SSD_CHUNK = 128
SSD_XBC = SSD_INNER + 2 * SSD_GROUPS * SSD_STATE
DT_PAD = 128

MOBA_HEADS = 8
MOBA_HEAD_DIM = 128
MOBA_INNER = 1024
MOBA_BLOCK = 256
MOBA_TOPK = 3
REL_BUCKETS = 32
REL_MAX_DIST = 1024
MOBA_NEAR = 5
MOBA_QBLOCKS = 2
MOBA_CHAIN_BATCH = 2
MOBA_CHAIN_HEADS = 2
LOG2E = math.log2(math.e)

RWKV_HEADS = 16
RWKV_HEAD_DIM = 64
RWKV_INNER = 1024
RWKV_DECAY_LORA = 64
RWKV_ICLR_LORA = 64
RWKV_GATE_LORA = 128
RWKV_IN = 3 * RWKV_INNER + RWKV_DECAY_LORA + RWKV_ICLR_LORA + RWKV_GATE_LORA
RWKV_LN_EPS = 64e-5
RWKV_CHUNK = 64
RWKV_GROUP = 4

PEER_HEADS = 8
PEER_KEYS = 128
PEER_TOPK = 16
PEER_KEY_DIM = 256

IN_SSD = SSD_INNER + SSD_XBC + SSD_HEADS
IN_MOBA = 3 * MOBA_INNER


def _dot(a, b, precision=None):
    return jnp.dot(a, b, preferred_element_type=F32, precision=precision)


def _dot_nt(a, b, precision=None):
    return lax.dot_general(a, b, (((1,), (1,)), ((), ())), preferred_element_type=F32, precision=precision)


def _dot_tn(a, b, precision=None):
    return lax.dot_general(a, b, (((0,), (0,)), ((), ())), preferred_element_type=F32, precision=precision)


def _split(x):
    hi = x.astype(BF16)
    return hi, (x - hi.astype(F32)).astype(BF16)


def _dot_fx(x, w):
    hi, lo = _split(x)
    return _dot(hi, w) + _dot(lo, w)


def _dot_xf(w, x):
    hi, lo = _split(x)
    return _dot(w, hi) + _dot(w, lo)


def _dot_ff(x, w, nt=False):
    dot = _dot_nt if nt else _dot
    xh, xl = _split(x)
    wh, wl = _split(w)
    return dot(xh, wh) + dot(xl, wh) + dot(xh, wl)


def _rms(x, g):
    return x * lax.rsqrt(jnp.mean(x * x, axis=-1, keepdims=True) + NORM_EPS) * g


def _silu(x):
    return x * jax.nn.sigmoid(x)


def _params(*sem):
    return pltpu.CompilerParams(dimension_semantics=sem, vmem_limit_bytes=VMEM_LIMIT)


def _norm_matmul_kernel(x_ref, g_ref, w_ref, o_ref, *, col_chunk):
    hb = _rms(x_ref[...], g_ref[...]).astype(BF16)
    for c in range(0, o_ref.shape[1], col_chunk):
        o_ref[:, c:c + col_chunk] = _dot(hb, w_ref[:, c:c + col_chunk])


def norm_matmul(x, g, w, tm=512, col_chunk=512):
    t, d = x.shape
    n = w.shape[1]
    col_chunk = math.gcd(n, col_chunk)
    return pl.pallas_call(
        functools.partial(_norm_matmul_kernel, col_chunk=col_chunk),
        grid=(t // tm,),
        in_specs=[pl.BlockSpec((tm, d), lambda i: (i, 0)),
                  pl.BlockSpec((1, d), lambda i: (0, 0)),
                  pl.BlockSpec((d, n), lambda i: (0, 0))],
        out_specs=pl.BlockSpec((tm, n), lambda i: (i, 0)),
        out_shape=jax.ShapeDtypeStruct((t, n), F32),
        compiler_params=_params("parallel"),
        name="norm_matmul",
    )(x, g.reshape(1, d), w)


def _merge_kernel(x_ref, ya_ref, yb_ref, yc_ref, g_ref, wg_ref, bg_ref, pa_ref, pb_ref, pc_ref, wo_ref, o_ref):
    x = x_ref[...]
    hb = _rms(x, g_ref[...]).astype(BF16)
    d = x.shape[1]
    merged = None
    for i, (y_ref, p_ref) in enumerate(((ya_ref, pa_ref), (yb_ref, pb_ref), (yc_ref, pc_ref))):
        gate = jax.nn.sigmoid(_dot(hb, wg_ref[:, i * d:(i + 1) * d]) + bg_ref[:, i * d:(i + 1) * d])
        term = gate * _dot(y_ref[...], p_ref[...])
        merged = term if merged is None else merged + term
    o_ref[...] = x + _dot(merged.astype(BF16), wo_ref[...])


def merge(x, ya, yb, yc, g, w_gate, b_gate, pa, pb, pc, w_out, tm=256):
    t, d = x.shape
    row = lambda i: (i, 0)
    fixed = lambda i: (0, 0)
    return pl.pallas_call(
        _merge_kernel,
        grid=(t // tm,),
        in_specs=[pl.BlockSpec((tm, d), row), pl.BlockSpec((tm, d), row), pl.BlockSpec((tm, d), row),
                  pl.BlockSpec((tm, d), row), pl.BlockSpec((1, d), fixed),
                  pl.BlockSpec((d, 3 * d), fixed), pl.BlockSpec((1, 3 * d), fixed),
                  pl.BlockSpec((d, d), fixed), pl.BlockSpec((d, d), fixed), pl.BlockSpec((d, d), fixed),
                  pl.BlockSpec((d, d), fixed)],
        out_specs=pl.BlockSpec((tm, d), row),
        out_shape=jax.ShapeDtypeStruct((t, d), F32),
        compiler_params=_params("parallel"),
        name="merge",
    )(x, ya, yb, yc, g.reshape(1, d), w_gate, b_gate.reshape(1, 3 * d), pa, pb, pc, w_out)


def _ssd_kernel(xbc_ref, z_ref, dt_ref, cw_ref, cb_ref, dtb_ref, a_ref, dsk_ref, ng_ref, rexp_ref, tri_ref,
                o_ref, buf_ref, state_ref):
    lc = SSD_CHUNK
    c = pl.program_id(1)

    @pl.when(c == 0)
    def _():
        buf_ref[0:8, :] = jnp.zeros((8, SSD_XBC), F32)
        state_ref[...] = jnp.zeros_like(state_ref)

    cur = xbc_ref[...]
    buf_ref[8:8 + lc, :] = cur
    acc = cb_ref[...] + cw_ref[0:1, :] * buf_ref[5:5 + lc, :]
    for k in range(1, SSD_CONV):
        acc = acc + cw_ref[k:k + 1, :] * buf_ref[5 + k:5 + k + lc, :]
    buf_ref[0:8, :] = cur[lc - 8:lc, :]
    xbc = _silu(acc)
    xs = xbc[:, :SSD_INNER]

    dt = jax.nn.softplus(dt_ref[...] + dtb_ref[...])
    acs = _dot_xf(tri_ref[...], dt * a_ref[...])
    acs_t = acs.T
    rexp = rexp_ref[...]
    dt_x = _dot_fx(dt, rexp)
    e_x = _dot_fx(jnp.exp(acs), rexp)
    dte_x = _dot_fx(jnp.exp(acs[lc - 1:lc, :] - acs), rexp)
    xdt = xs * dt_x
    xdt_b = xdt.astype(BF16)
    xw_b = (xdt * dte_x).astype(BF16)
    chunk_decay = e_x[lc - 1:lc, :]

    row = lax.broadcasted_iota(jnp.int32, (lc, lc), 0)
    col = lax.broadcasted_iota(jnp.int32, (lc, lc), 1)
    causal = row >= col
    gw = SSD_INNER // SSD_GROUPS
    hpg = SSD_HEADS // SSD_GROUPS
    n = SSD_STATE
    ys = []
    for g in range(SSD_GROUPS):
        b_g = xbc[:, SSD_INNER + g * n:SSD_INNER + (g + 1) * n].astype(BF16)
        c_g = xbc[:, SSD_INNER + (SSD_GROUPS + g) * n:SSD_INNER + (SSD_GROUPS + g + 1) * n].astype(BF16)
        cb = _dot_nt(c_g, b_g)
        s_in = state_ref[:, g * gw:(g + 1) * gw]
        y_off = _dot(c_g, s_in.astype(BF16)) * e_x[:, g * gw:(g + 1) * gw]
        state_ref[:, g * gw:(g + 1) * gw] = (chunk_decay[:, g * gw:(g + 1) * gw] * s_in
                                             + _dot_tn(b_g, xw_b[:, g * gw:(g + 1) * gw]))
        diag = []
        for e in range(hpg):
            h = g * hpg + e
            seg = jnp.exp(jnp.where(causal, acs[:, h:h + 1] - acs_t[h:h + 1, :], NEG))
            m = (cb * seg).astype(BF16)
            diag.append(_dot(m, xdt_b[:, h * SSD_HEAD_DIM:(h + 1) * SSD_HEAD_DIM]))
        ys.append(jnp.concatenate(diag, axis=-1) + y_off)
    y = jnp.concatenate(ys, axis=-1) + xs * dsk_ref[...]
    y = y * _silu(z_ref[...])
    for g in range(SSD_GROUPS):
        yg = y[:, g * gw:(g + 1) * gw]
        yg = yg * lax.rsqrt(jnp.mean(yg * yg, axis=-1, keepdims=True) + NORM_EPS)
        o_ref[:, g * gw:(g + 1) * gw] = (yg * ng_ref[:, g * gw:(g + 1) * gw]).astype(o_ref.dtype)


def ssd(proj, bsz, slen, conv_w, conv_b, dt_bias, a_log, d_skip, norm_g):
    lc = SSD_CHUNK
    nc = slen // lc
    t = bsz * slen
    pad = DT_PAD - SSD_HEADS
    dtb = jnp.pad(dt_bias.astype(F32), (0, pad)).reshape(1, DT_PAD)
    a = jnp.pad(-jnp.exp(a_log.astype(F32)), (0, pad)).reshape(1, DT_PAD)
    dsk = jnp.repeat(d_skip.astype(F32), SSD_HEAD_DIM).reshape(1, SSD_INNER)
    rexp = jnp.asarray((np.arange(DT_PAD)[:, None] == np.arange(SSD_INNER)[None, :] // SSD_HEAD_DIM)
                       .astype(np.float32), dtype=BF16)
    tri = jnp.asarray(np.tril(np.ones((lc, lc), np.float32)), dtype=BF16)
    rowmap = lambda b, c: (b * nc + c, 0)
    fixed = lambda b, c: (0, 0)
    zblk = SSD_XBC // SSD_INNER
    dtblk = (SSD_XBC + SSD_INNER) // DT_PAD
    return pl.pallas_call(
        _ssd_kernel,
        grid=(bsz, nc),
        in_specs=[pl.BlockSpec((lc, SSD_XBC), rowmap),
                  pl.BlockSpec((lc, SSD_INNER), lambda b, c: (b * nc + c, zblk)),
                  pl.BlockSpec((lc, DT_PAD), lambda b, c: (b * nc + c, dtblk)),
                  pl.BlockSpec((SSD_CONV, SSD_XBC), fixed), pl.BlockSpec((1, SSD_XBC), fixed),
                  pl.BlockSpec((1, DT_PAD), fixed), pl.BlockSpec((1, DT_PAD), fixed),
                  pl.BlockSpec((1, SSD_INNER), fixed), pl.BlockSpec((1, SSD_INNER), fixed),
                  pl.BlockSpec((DT_PAD, SSD_INNER), fixed), pl.BlockSpec((lc, lc), fixed)],
        out_specs=pl.BlockSpec((lc, SSD_INNER), rowmap),
        out_shape=jax.ShapeDtypeStruct((t, SSD_INNER), BF16),
        scratch_shapes=[pltpu.VMEM((lc + 8, SSD_XBC), F32), pltpu.VMEM((SSD_STATE, SSD_INNER), F32)],
        compiler_params=_params("parallel", "arbitrary"),
        name="ssd",
    )(proj, proj, proj, conv_w.astype(F32), conv_b.astype(F32).reshape(1, SSD_XBC), dtb, a, dsk,
      norm_g.astype(F32).reshape(1, SSD_INNER), rexp, tri)


def _ssd_weight(w_in_l):
    w = w_in_l[:, :IN_SSD]
    z, xbc, dt = w[:, :SSD_INNER], w[:, SSD_INNER:SSD_INNER + SSD_XBC], w[:, SSD_INNER + SSD_XBC:]
    dt = jnp.pad(dt, ((0, 0), (0, DT_PAD - SSD_HEADS)))
    return jnp.concatenate([xbc, z, dt], axis=1).astype(BF16)


def _t5_bucket(dist):
    max_exact = REL_BUCKETS // 2
    d = jnp.maximum(dist, 0)
    df = jnp.maximum(d, 1).astype(F32)
    large = max_exact + (jnp.log(df / max_exact) / math.log(REL_MAX_DIST / max_exact)
                         * (REL_BUCKETS - max_exact)).astype(jnp.int32)
    large = jnp.minimum(large, REL_BUCKETS - 1)
    return jnp.where(d < max_exact, d, large)


def _bias_table_kernel(bias_ref, bucket_ref, o_ref):
    h = pl.program_id(0)
    for d in range(MOBA_NEAR):
        bk = bucket_ref[d]
        acc = jnp.where(bk < 0, NEG, 0.0)
        for b in range(REL_BUCKETS):
            acc = jnp.where(bk == b, bias_ref[h, b] * LOG2E, acc)
        o_ref[d] = acc
    o_ref[MOBA_NEAR] = jnp.zeros(o_ref.shape[1:], F32)


def moba_bias_table(rel_bias):
    blk = MOBA_BLOCK
    q = jnp.arange(blk, dtype=jnp.int32)[None, :, None]
    k = jnp.arange(blk, dtype=jnp.int32)[None, None, :]
    dist = jnp.arange(MOBA_NEAR, dtype=jnp.int32)[:, None, None] * blk + q - k
    bucket = jnp.where(dist < 0, -1, _t5_bucket(dist))
    return pl.pallas_call(
        _bias_table_kernel,
        grid=(MOBA_HEADS,),
        in_specs=[pl.BlockSpec(memory_space=pltpu.SMEM),
                  pl.BlockSpec((MOBA_NEAR, blk, blk), lambda h: (0, 0, 0))],
        out_specs=pl.BlockSpec((None, MOBA_NEAR + 1, blk, blk), lambda h: (h, 0, 0, 0)),
        out_shape=jax.ShapeDtypeStruct((MOBA_HEADS, MOBA_NEAR + 1, blk, blk), F32),
        compiler_params=_params("arbitrary"),
        name="moba_bias_table",
    )(rel_bias.astype(F32).T, bucket)


def _moba_prep_kernel(q_ref, k_ref, v_ref, qg_ref, kg_ref, far_ref, qa_ref, kb_ref, vb_ref, kmean_ref):
    i = pl.program_id(2)
    blk, dh = q_ref.shape
    half = dh // 2

    @pl.when(i == 0)
    def _():
        kmean_ref[...] = jnp.zeros_like(kmean_ref)

    qn = _rms(q_ref[...], qg_ref[...])
    kn = _rms(k_ref[...], kg_ref[...])
    gate = _dot_nt(qn, kmean_ref[...], HIGHEST)
    col = lax.broadcasted_iota(jnp.int32, (blk, dh), 1)
    colf = col.astype(F32)
    valid = col < i
    g = jnp.where(valid, gate, NEG)
    sel = col < 0
    for _ in range(MOBA_TOPK):
        m = jnp.max(g, axis=-1, keepdims=True)
        idx = jnp.min(jnp.where(g == m, colf, float(dh)), axis=-1, keepdims=True)
        hit = colf == idx
        sel = sel | hit
        g = jnp.where(hit, -jnp.inf, g)
    sel = sel & valid

    far = far_ref[...] * LOG2E
    far_hi = far.astype(BF16).astype(F32)
    far_lo = far - far_hi
    sel_f = jnp.where(sel, 1.0, 0.0)
    low = col >= half
    sel2 = jnp.where(low, pltpu.roll(sel_f, half, 1), sel_f) > 0.5
    jcol = jnp.where(low, col - half, col)
    is_far = (i - jcol) >= MOBA_NEAR
    pen_hi = jnp.where(jcol == i, 0.0, jnp.where(sel2, jnp.where(is_far, far_hi, 0.0), NEG))
    pen_lo = jnp.where(sel2 & is_far, far_lo, 0.0)
    qa_ref[:, 0:dh] = (qn * (dh ** -0.5 * LOG2E)).astype(BF16)
    qa_ref[:, dh:2 * dh] = jnp.where(low, pen_lo, pen_hi).astype(BF16)
    kb_ref[...] = kn.astype(BF16)
    vb_ref[...] = v_ref[...].astype(BF16)
    kmean_ref[pl.ds(i, 1), :] = jnp.mean(kn, axis=0, keepdims=True)


def _moba_attn_kernel(qa_ref, kb_ref, vb_ref, tab_ref, o_ref, m_ref, acc_ref):
    a = pl.program_id(2)
    blk = MOBA_BLOCK
    dh = m_ref.shape[-1]
    chains = [(bi, hi) for bi in range(qa_ref.shape[0]) for hi in range(qa_ref.shape[1])]
    q = [qa_ref[bi, hi] for bi, hi in chains]
    m_ref[...] = jnp.full_like(m_ref, 0.1 * NEG)
    acc_ref[...] = jnp.zeros_like(acc_ref)

    def step(off, nk, tabs):
        row = lax.broadcasted_iota(jnp.int32, (nk, dh), 0)
        lane = lax.broadcasted_iota(jnp.int32, (nk, dh), 1)
        key_block = off // blk + lax.shift_right_logical(row, blk.bit_length() - 1)
        onehot = jnp.where((lane & (dh // 2 - 1)) == key_block, 1.0, 0.0).astype(BF16)
        ones = jnp.ones((nk, dh), BF16)
        s = [_dot_nt(q[n], jnp.concatenate([kb_ref[bi, hi, pl.ds(off, nk), :], onehot], axis=1))
             for n, (bi, hi) in enumerate(chains)]
        if tabs is not None:
            s = [x + tabs[hi] for x, (_, hi) in zip(s, chains)]
        m_old = [m_ref[n] for n in range(len(chains))]
        m_new = [jnp.maximum(mo, jnp.max(x, axis=-1, keepdims=True)) for mo, x in zip(m_old, s)]
        alpha = [jnp.exp2(mo - mn) for mo, mn in zip(m_old, m_new)]
        p = [jnp.exp2(x - jnp.concatenate([mn] * (nk // dh), axis=-1)).astype(BF16) for x, mn in zip(s, m_new)]
        pv = [_dot(p[n], jnp.concatenate([vb_ref[bi, hi, pl.ds(off, nk), :], ones], axis=1))
              for n, (bi, hi) in enumerate(chains)]
        for n in range(len(chains)):
            acc_ref[n] = jnp.concatenate([alpha[n], alpha[n]], axis=-1) * acc_ref[n] + pv[n]
            m_ref[n] = m_new[n]

    first = MOBA_QBLOCKS * a
    n_far = jnp.maximum(first - (MOBA_NEAR - 1), 0)

    def far_body(t, c):
        step(pl.multiple_of(t * (MOBA_QBLOCKS * blk), MOBA_QBLOCKS * blk), MOBA_QBLOCKS * blk, None)
        return c

    def near_body(j, c):
        tabs = [jnp.concatenate([tab_ref[hi, jnp.clip(first + r - j, 0, MOBA_NEAR)] for r in range(MOBA_QBLOCKS)],
                                axis=0) for hi in range(qa_ref.shape[1])]
        step(pl.multiple_of(j * blk, blk), blk, tabs)
        return c

    lax.fori_loop(0, n_far // MOBA_QBLOCKS, far_body, 0)
    lax.fori_loop(n_far, first + MOBA_QBLOCKS, near_body, 0)
    for n, (bi, hi) in enumerate(chains):
        acc = acc_ref[n]
        o_ref[bi, :, hi * dh:(hi + 1) * dh] = (acc[:, :dh] / acc[:, dh:]).astype(o_ref.dtype)


def moba(proj, bsz, slen, q_norm_g, k_norm_g, bias_table, far_bias):
    blk, dh, nh = MOBA_BLOCK, MOBA_HEAD_DIM, MOBA_HEADS
    nb = slen // blk
    tq = MOBA_QBLOCKS * blk
    assert slen % tq == 0 and nb <= dh // 2 and (MOBA_NEAR - 1) % MOBA_QBLOCKS == 0
    t = bsz * slen
    far = jnp.broadcast_to(far_bias.astype(F32)[:, None, None], (nh, 1, dh))
    per_block = lambda width: pl.BlockSpec((None, None, blk, width), lambda b, h, i: (b, h, i, 0))
    qa, kb, vb = pl.pallas_call(
        _moba_prep_kernel,
        grid=(bsz, nh, nb),
        in_specs=[pl.BlockSpec((blk, dh), lambda b, h, i: (b * nb + i, h)),
                  pl.BlockSpec((blk, dh), lambda b, h, i: (b * nb + i, nh + h)),
                  pl.BlockSpec((blk, dh), lambda b, h, i: (b * nb + i, 2 * nh + h)),
                  pl.BlockSpec((1, dh), lambda b, h, i: (0, 0)),
                  pl.BlockSpec((1, dh), lambda b, h, i: (0, 0)),
                  pl.BlockSpec((None, 1, dh), lambda b, h, i: (h, 0, 0))],
        out_specs=[per_block(2 * dh), per_block(dh), per_block(dh)],
        out_shape=[jax.ShapeDtypeStruct((bsz, nh, slen, 2 * dh), BF16),
                   jax.ShapeDtypeStruct((bsz, nh, slen, dh), BF16),
                   jax.ShapeDtypeStruct((bsz, nh, slen, dh), BF16)],
        scratch_shapes=[pltpu.VMEM((dh, dh), F32)],
        compiler_params=_params("parallel", "parallel", "arbitrary"),
        name="moba_prep",
    )(proj, proj, proj, q_norm_g.astype(F32).reshape(1, dh), k_norm_g.astype(F32).reshape(1, dh), far)
    cb, chd = MOBA_CHAIN_BATCH, MOBA_CHAIN_HEADS
    assert bsz % cb == 0 and nh % chd == 0
    whole = pl.BlockSpec((cb, chd, slen, dh), lambda b, h, a: (b, h, 0, 0), pipeline_mode=pl.Buffered(1))
    out = pl.pallas_call(
        _moba_attn_kernel,
        grid=(bsz // cb, nh // chd, slen // tq),
        in_specs=[pl.BlockSpec((cb, chd, tq, 2 * dh), lambda b, h, a: (b, h, a, 0)), whole, whole,
                  pl.BlockSpec((chd, MOBA_NEAR + 1, blk, blk), lambda b, h, a: (h, 0, 0, 0))],
        out_specs=pl.BlockSpec((cb, tq, chd * dh), lambda b, h, a: (b, a, h)),
        out_shape=jax.ShapeDtypeStruct((bsz, slen, nh * dh), BF16),
        scratch_shapes=[pltpu.VMEM((cb * chd, tq, dh), F32), pltpu.VMEM((cb * chd, tq, 2 * dh), F32)],
        compiler_params=_params("parallel", "parallel", "arbitrary"),
        name="moba_attn",
    )(qa, kb, vb, bias_table)
    return out.reshape(t, nh * dh)


def _rwkv_prep_kernel(cur_ref, prev_ref, mu_ref, w0_ref, ww2_ref, a0_ref, wa2_ref, wg2_ref, kk_ref, ka_ref,
                      hsum_ref, hexp_ref, r_ref, lw_ref, k_ref, v_ref, a_ref, b_ref, g_ref, buf_ref,
                      *, tiles_per_seq):
    i = pl.program_id(0)
    tm = cur_ref.shape[0]
    c = RWKV_INNER
    cur = cur_ref[...]
    first = (i % tiles_per_seq) == 0
    buf_ref[8:8 + tm, :] = cur
    buf_ref[0:8, :] = jnp.where(first, 0.0, prev_ref[...])
    prev = buf_ref[7:7 + tm, :]
    xs = cur + (prev - cur) * mu_ref[...]
    r, k, v = xs[:, :c], xs[:, c:2 * c], xs[:, 2 * c:3 * c]
    o1 = 3 * c + RWKV_DECAY_LORA
    o2 = o1 + RWKV_ICLR_LORA
    wl, al, gl = xs[:, 3 * c:o1], xs[:, o1:o2], xs[:, o2:]
    w = -jax.nn.softplus(-(w0_ref[...] + _dot_ff(jnp.tanh(wl), ww2_ref[...]))) - 0.5
    a = jax.nn.sigmoid(a0_ref[...] + _dot_ff(al, wa2_ref[...]))
    kk = k * kk_ref[...]
    ss = _dot_fx(_dot_fx(kk * kk, hsum_ref[...]), hexp_ref[...])
    kk = kk * lax.rsqrt(jnp.maximum(ss, 1e-12))
    r_ref[...] = r
    lw_ref[...] = -jnp.exp(w)
    k_ref[...] = k * (1.0 + (a - 1.0) * ka_ref[...])
    v_ref[...] = v
    a_ref[...] = -kk
    b_ref[...] = kk * a
    g_ref[...] = _dot_ff(jax.nn.sigmoid(gl), wg2_ref[...])


def _bd_rows(y, mask):
    return jnp.where(mask, jnp.concatenate([y] * RWKV_GROUP, axis=0), 0.0)


def _rwkv_scan_kernel(r_ref, lw_ref, k_ref, v_ref, a_ref, b_ref, g_ref, rk_ref, lng_ref, lnb_ref,
                      tri_ref, lows_ref, lowi_ref, eye_ref, bd_ref, avg_ref, o_ref, s_ref, *, mxu_dtype):
    ci = pl.program_id(1)
    ch = RWKV_CHUNK

    @pl.when(ci == 0)
    def _():
        s_ref[...] = jnp.zeros_like(s_ref)

    prec = HIGHEST if mxu_dtype == F32 else None

    def mm(x, y):
        return _dot(x.astype(mxu_dtype), y.astype(mxu_dtype), prec)

    def mm_nt(x, y):
        return _dot_nt(x.astype(mxu_dtype), y.astype(mxu_dtype), prec)

    def mm_tn(x, y):
        return _dot_tn(x.astype(mxu_dtype), y.astype(mxu_dtype), prec)

    bd = bd_ref[...] > 0.5
    strict = lows_ref[...] > 0.5
    incl = lowi_ref[...] > 0.5
    gw = bd_ref.shape[0]
    eye_big = lax.broadcasted_iota(jnp.int32, (gw, gw), 0) == lax.broadcasted_iota(jnp.int32, (gw, gw), 1)
    avg = avg_ref[...]

    bsz = r_ref.shape[0]
    chains = [(bi, gi) for bi in range(bsz) for gi in range(r_ref.shape[2] // gw)]
    cols = [slice(gi * gw, (gi + 1) * gw) for _, gi in chains]

    def each(fn, *lists):
        return [fn(*xs) for xs in zip(*lists)]

    def bdr(ys):
        return [_bd_rows(y, bd) for y in ys]

    r, lw, k, v, a, b = ([ref[bi, :, c] for (bi, _), c in zip(chains, cols)]
                         for ref in (r_ref, lw_ref, k_ref, v_ref, a_ref, b_ref))
    cl = each(lambda x: _dot_xf(tri_ref[...], x), lw)
    cl_end = [x[ch - 1:ch, :] for x in cl]
    e_neg = [jnp.exp(-x) for x in cl]
    a_t = each(lambda x, c, l: x * jnp.exp(c - l), a, cl, lw)
    r_t = each(lambda x, c: x * jnp.exp(c), r, cl)
    b_t = each(jnp.multiply, b, e_neg)
    k_t = each(jnp.multiply, k, e_neg)
    e_end = each(lambda ce, c: jnp.exp(ce - c), cl_end, cl)
    b_h = each(jnp.multiply, b, e_end)
    k_h = each(jnp.multiply, k, e_end)

    ar = each(lambda x, y: jnp.concatenate([x, y], axis=0), a_t, r_t)
    gb = each(mm_nt, ar, bdr(b_t))
    gk = each(mm_nt, ar, bdr(k_t))
    a_ab = [jnp.where(strict, x[:ch], 0.0) for x in gb]
    a_ak = [jnp.where(strict, x[:ch], 0.0) for x in gk]
    q_rb = [jnp.where(incl, x[ch:], 0.0) for x in gb]
    q_rk = [jnp.where(incl, x[ch:], 0.0) for x in gk]

    t_inv = [eye_ref[...] + x for x in a_ab]
    p = a_ab
    for _ in range(int(math.log2(ch)) - 1):
        p = each(mm, p, bdr(p))
        t_inv = each(lambda t, tp: t + tp, t_inv, each(mm, t_inv, bdr(p)))

    vbd = bdr(v)
    w_t = each(mm, t_inv, bdr(a_t))
    u0 = each(mm, t_inv, bdr(each(mm, a_ak, vbd)))
    z = each(lambda x, y: x + y, r_t, each(mm, q_rb, bdr(w_t)))
    y_loc = each(lambda x, y: x + y, each(mm, q_rb, bdr(u0)), each(mm, q_rk, vbd))
    bwu = each(lambda x, w, u: mm_tn(x, jnp.concatenate([w, u], axis=1)), b_h, w_t, u0)
    kv = each(mm_tn, k_h, v)
    m_s = each(lambda x, ce: jnp.where(bd, x[:, :gw], 0.0) + jnp.where(eye_big, jnp.exp(ce), 0.0), bwu, cl_end)
    n_s = each(lambda x, y: jnp.where(bd, x[:, gw:] + y, 0.0), bwu, kv)
    s0 = [s_ref[n] for n in range(len(chains))]
    zs = each(lambda x, m, s: mm(jnp.concatenate([x, m], axis=0), s), z, m_s, s0)
    y = each(lambda x, yl: x[:ch] + yl, zs, y_loc)
    for n in range(len(chains)):
        s_ref[n] = zs[n][ch:] + n_s[n]

    mean = [_dot_fx(x, avg) for x in y]
    dlt = each(lambda x, m: x - m, y, mean)
    var = [_dot_fx(x * x, avg) for x in dlt]
    rkk = each(lambda x, kk, c: _dot_fx(x * kk * rk_ref[:, c], avg), r, k, cols)
    for n, ((bi, _), c) in enumerate(zip(chains, cols)):
        yn = dlt[n] * lax.rsqrt(var[n] + RWKV_LN_EPS) * lng_ref[:, c] + lnb_ref[:, c]
        bonus = rkk[n] * float(RWKV_HEAD_DIM) * v[n]
        o_ref[bi, :, c] = ((yn + bonus) * g_ref[bi, :, c]).astype(o_ref.dtype)


def rwkv(proj, bsz, slen, mu, w0, w_w2, a0, w_a2, w_g2, k_k, k_a, r_k, lnx_g, lnx_b, tm=256, mxu_dtype=BF16,
         groups_per_step=4):
    c, ch, grp, n = RWKV_INNER, RWKV_CHUNK, RWKV_GROUP, RWKV_HEAD_DIM
    t = bsz * slen
    f = lambda p: p.astype(F32).reshape(1, -1)
    heads = np.arange(c) // n
    hsum = jnp.asarray((heads[:, None] == np.arange(128)[None, :]).astype(np.float32), dtype=BF16)
    hexp = hsum.T
    row = lambda i: (i, 0)
    fixed = lambda i: (0, 0)
    wide = pl.BlockSpec((tm, c), row)
    vec = pl.BlockSpec((1, c), fixed)
    r, lw, k, v, a, b, g = pl.pallas_call(
        functools.partial(_rwkv_prep_kernel, tiles_per_seq=slen // tm),
        grid=(t // tm,),
        in_specs=[pl.BlockSpec((tm, RWKV_IN), row),
                  pl.BlockSpec((8, RWKV_IN), lambda i: (jnp.maximum(i * (tm // 8) - 1, 0), 0)),
                  pl.BlockSpec((1, RWKV_IN), fixed), vec,
                  pl.BlockSpec((RWKV_DECAY_LORA, c), fixed), vec,
                  pl.BlockSpec((RWKV_ICLR_LORA, c), fixed), pl.BlockSpec((RWKV_GATE_LORA, c), fixed),
                  vec, vec, pl.BlockSpec((c, 128), fixed), pl.BlockSpec((128, c), fixed)],
        out_specs=[wide] * 7,
        out_shape=[jax.ShapeDtypeStruct((t, c), F32)] * 7,
        scratch_shapes=[pltpu.VMEM((tm + 8, RWKV_IN), F32)],
        compiler_params=_params("parallel"),
        name="rwkv_prep",
    )(proj, proj, f(mu), f(w0), w_w2.astype(F32), f(a0), w_a2.astype(F32), w_g2.astype(F32), f(k_k), f(k_a),
      hsum, hexp)

    gw = grp * n
    nc = slen // ch
    tt = np.arange(ch)
    colt = np.arange(gw) % ch
    hrow = np.arange(gw) // n
    tri = jnp.asarray((tt[:, None] >= tt[None, :]).astype(np.float32), dtype=BF16)
    lows = jnp.asarray((colt[None, :] < tt[:, None]).astype(np.float32))
    lowi = jnp.asarray((colt[None, :] <= tt[:, None]).astype(np.float32))
    eye = jnp.asarray((colt[None, :] == tt[:, None]).astype(np.float32))
    bdm = (hrow[:, None] == hrow[None, :]).astype(np.float32)
    bd = jnp.asarray(bdm)
    avg = jnp.asarray(bdm / n, dtype=BF16)
    sw = groups_per_step * gw
    blk = pl.BlockSpec((bsz, ch, sw), lambda gi, ci: (0, ci, gi))
    gvec = pl.BlockSpec((1, sw), lambda gi, ci: (0, gi))
    const = lambda shape: pl.BlockSpec(shape, lambda gi, ci: (0, 0))
    seq = lambda x: x.reshape(bsz, slen, c)
    out = pl.pallas_call(
        functools.partial(_rwkv_scan_kernel, mxu_dtype=mxu_dtype),
        grid=(c // sw, nc),
        in_specs=[blk] * 7 + [gvec] * 3 + [const((ch, ch)), const((ch, gw)), const((ch, gw)), const((ch, gw)),
                                           const((gw, gw)), const((gw, gw))],
        out_specs=blk,
        out_shape=jax.ShapeDtypeStruct((bsz, slen, c), BF16),
        scratch_shapes=[pltpu.VMEM((bsz * groups_per_step, gw, gw), F32)],
        compiler_params=_params("parallel", "arbitrary"),
        name="rwkv_scan",
    )(seq(r), seq(lw), seq(k), seq(v), seq(a), seq(b), seq(g), f(r_k), f(lnx_g), f(lnx_b),
      tri, lows, lowi, eye, bd, avg)
    return out.reshape(t, c)


PEER_RANKS = PEER_TOPK + 1
PEER_PAIRS = [(p, q) for p in range(PEER_RANKS) for q in range(PEER_RANKS // (p + 1))]
PEER_CAND_ROWS = -(-len(PEER_PAIRS) // 8) * 8
PEER_TOP_ROWS = -(-2 * PEER_RANKS // 8) * 8


def _peer_kernel(x_ref, g_ref, wq_ref, k1_ref, k2_ref, u_ref, vt_ref, o_ref,
                 h2_ref, thr_ref, e1_ref, e2_ref, acc_ref, top_ref, cand_ref, hu0_ref, hu1_ref, p0_ref, p1_ref):
    e = pl.program_id(1)
    n_blocks = 2 * (pl.num_programs(1) - 1)
    nk = PEER_KEYS
    half = PEER_KEY_DIM // 2
    eb = u_ref.shape[0] // 2

    @pl.when(e == 0)
    def _():
        for ref in (hu0_ref, hu1_ref, p0_ref, p1_ref):
            ref[...] = jnp.zeros_like(ref)
        hb = _rms(x_ref[...], g_ref[...]).astype(BF16)
        h2_ref[...] = hb
        q = _dot(hb, wq_ref[...])
        cand_ref[...] = jnp.full_like(cand_ref, -jnp.inf)
        for h in range(PEER_HEADS):
            base = h * PEER_KEY_DIM
            s1 = _dot_ff(k1_ref[...], q[:, base:base + half], nt=True)
            s2 = _dot_ff(k2_ref[...], q[:, base + half:base + 2 * half], nt=True)
            for side, s in enumerate((s1, s2)):
                cur = s
                for rnk in range(PEER_RANKS):
                    m = jnp.max(cur, axis=0, keepdims=True)
                    top_ref[side * PEER_RANKS + rnk:side * PEER_RANKS + rnk + 1, :] = m
                    cur = jnp.where(cur == m, -jnp.inf, cur)
            for ci, (p, qq) in enumerate(PEER_PAIRS):
                cand_ref[ci:ci + 1, :] = top_ref[p:p + 1, :] + top_ref[PEER_RANKS + qq:PEER_RANKS + qq + 1, :]
            cand = cand_ref[...]
            cur = cand
            best = jnp.max(cur, axis=0, keepdims=True)
            tau = best
            for _ in range(PEER_TOPK - 1):
                cur = jnp.where(cur == tau, -jnp.inf, cur)
                tau = jnp.max(cur, axis=0, keepdims=True)
            nxt = jnp.max(jnp.where(cur == tau, -jnp.inf, cur), axis=0, keepdims=True)
            zsum = jnp.sum(jnp.where(cand >= tau, jnp.exp(cand - best), 0.0), axis=0, keepdims=True)
            b0 = top_ref[PEER_RANKS:PEER_RANKS + 1, :]
            thr_ref[h] = jnp.exp(0.5 * (tau + nxt) - s1 - b0)
            e1_ref[h] = jnp.exp(s1 - top_ref[0:1, :]) / zsum
            e2_ref[h] = jnp.exp(s2 - b0)
        acc_ref[...] = jnp.zeros_like(acc_ref)

    hu_slots, p_slots = (hu0_ref, hu1_ref), (p0_ref, p1_ref)
    tt = h2_ref.shape[0]
    d = acc_ref.shape[0]
    tok = [slice(n * (tt // 2), (n + 1) * (tt // 2)) for n in range(2)]
    for par in range(2):
        tick = 2 * e + par
        blk_b = jnp.clip(tick - 1, 0, n_blocks - 1)
        p_rd, p_wr, hu_rd, hu_wr = p_slots[par], p_slots[1 - par], hu_slots[1 - par], hu_slots[par]

        def down(n, r):
            rows = slice(r * (d // 2), (r + 1) * (d // 2))
            acc_ref[rows, tok[n]] += _dot(vt_ref[rows, par * eb:(par + 1) * eb], p_rd[:, tok[n]])

        def up(n, r):
            rows = slice(r * (eb // 2), (r + 1) * (eb // 2))
            hu_wr[rows, tok[n]] = _dot_nt(u_ref[par * eb + r * (eb // 2):par * eb + (r + 1) * (eb // 2), :],
                                          h2_ref[tok[n], :])

        def weigh(n, ib):
            rows = slice(ib * nk, (ib + 1) * nk)
            i = blk_b * (eb // nk) + ib
            hu = hu_rd[rows, tok[n]]
            act = 0.5 * hu * (1.0 + lax.erf(hu * (2.0 ** -0.5)))
            w = None
            for h in range(PEER_HEADS):
                e2 = e2_ref[h, :, tok[n]]
                term = jnp.where(e2 >= thr_ref[h, pl.ds(i, 1), tok[n]], e2 * e1_ref[h, pl.ds(i, 1), tok[n]], 0.0)
                w = term if w is None else w + term
            p_wr[rows, tok[n]] = (w * act).astype(BF16)

        matmul_jobs = [(fn, n, r) for n in range(2) for r in range(2) for fn in (down, up)]
        vector_jobs = [(n, ib) for n in range(2) for ib in range(eb // nk)]
        assert len(matmul_jobs) == len(vector_jobs)
        for (fn, n, r), (vn, ib) in zip(matmul_jobs, vector_jobs):
            fn(n, r)
            weigh(vn, ib)

    @pl.when(e == pl.num_programs(1) - 1)
    def _():
        o_ref[...] = x_ref[...] + acc_ref[...].T


def peer(x, g, wq, k1, k2, u, vt, tt=512, eb=512):
    t, d = x.shape
    ne = u.shape[0]
    nh, nk = PEER_HEADS, PEER_KEYS
    nb2 = ne // (2 * eb)
    assert ne % (2 * eb) == 0
    return pl.pallas_call(
        _peer_kernel,
        grid=(t // tt, nb2 + 1),
        in_specs=[pl.BlockSpec((tt, d), lambda i, e: (i, 0)),
                  pl.BlockSpec((1, d), lambda i, e: (0, 0)),
                  pl.BlockSpec((d, nh * PEER_KEY_DIM), lambda i, e: (0, 0)),
                  pl.BlockSpec((nk, PEER_KEY_DIM // 2), lambda i, e: (0, 0)),
                  pl.BlockSpec((nk, PEER_KEY_DIM // 2), lambda i, e: (0, 0)),
                  pl.BlockSpec((2 * eb, d), lambda i, e: (jnp.minimum(e, nb2 - 1), 0)),
                  pl.BlockSpec((d, 2 * eb), lambda i, e: (0, jnp.maximum(e - 1, 0)))],
        out_specs=pl.BlockSpec((tt, d), lambda i, e: (i, 0)),
        out_shape=jax.ShapeDtypeStruct((t, d), F32),
        scratch_shapes=[pltpu.VMEM((tt, d), BF16),
                        pltpu.VMEM((nh, nk, tt), F32), pltpu.VMEM((nh, nk, tt), F32),
                        pltpu.VMEM((nh, nk, tt), F32),
                        pltpu.VMEM((d, tt), F32),
                        pltpu.VMEM((PEER_TOP_ROWS, tt), F32), pltpu.VMEM((PEER_CAND_ROWS, tt), F32),
                        pltpu.VMEM((eb, tt), F32), pltpu.VMEM((eb, tt), F32),
                        pltpu.VMEM((eb, tt), BF16), pltpu.VMEM((eb, tt), BF16)],
        compiler_params=_params("parallel", "arbitrary"),
        name="peer",
    )(x, g.astype(F32).reshape(1, d), wq, k1.astype(F32), k2.astype(F32), u, vt)


def kernel(x, rel_bias, norm1_g, w_in, conv_w, conv_b, dt_bias, a_log, d_skip, ssd_norm_g, q_norm_g, k_norm_g, rwkv_mu, w0, w_w2, a0, w_a2, w_g2, k_k, k_a, r_k, lnx_g, lnx_b, p_ssd, p_moba, p_rwkv, w_gate, b_gate, w_out, norm2_g, peer_wq, peer_k1, peer_k2, peer_u, peer_v):
    bsz, slen, d = x.shape
    t = bsz * slen
    xf = x.astype(F32).reshape(t, d)
    bias_table = moba_bias_table(rel_bias)
    far_bias = rel_bias[REL_BUCKETS - 1]
    for l in range(w_in.shape[0]):
        g1 = norm1_g[l].astype(F32)
        w_moba = w_in[l][:, IN_SSD:IN_SSD + IN_MOBA].astype(BF16)
        w_rwkv = w_in[l][:, IN_SSD + IN_MOBA:].astype(BF16)
        ya = ssd(norm_matmul(xf, g1, _ssd_weight(w_in[l])), bsz, slen, conv_w[l], conv_b[l], dt_bias[l],
                 a_log[l], d_skip[l], ssd_norm_g[l])
        yb = moba(norm_matmul(xf, g1, w_moba), bsz, slen, q_norm_g[l], k_norm_g[l], bias_table, far_bias)
        yc = rwkv(norm_matmul(xf, g1, w_rwkv), bsz, slen, rwkv_mu[l], w0[l], w_w2[l], a0[l], w_a2[l], w_g2[l],
                  k_k[l], k_a[l], r_k[l], lnx_g[l], lnx_b[l])
        xf = merge(xf, ya, yb, yc, g1, w_gate[l].astype(BF16), b_gate[l].astype(F32), p_ssd[l].astype(BF16),
                   p_moba[l].astype(BF16), p_rwkv[l].astype(BF16), w_out[l].astype(BF16))
        xf = peer(xf, norm2_g[l], peer_wq[l].astype(BF16), peer_k1[l], peer_k2[l], peer_u[l].astype(BF16),
                  peer_v[l].astype(BF16).T)
    return xf.reshape(bsz, slen, d).astype(x.dtype)
```

```python
import functools
import math

import numpy as np
import jax
import jax.numpy as jnp
from jax import lax
from jax.experimental import pallas as pl
from jax.experimental.pallas import tpu as pltpu

F32 = jnp.float32
BF16 = jnp.bfloat16
HIGHEST = lax.Precision.HIGHEST

D_MODEL = 1024
NORM_EPS = 1e-6
NEG = -1e30
VMEM_LIMIT = 56 * 1024 * 1024

SSD_HEADS = 16
SSD_HEAD_DIM = 64
SSD_INNER = 1024
SSD_GROUPS = 4
SSD_STATE = 128
SSD_CONV = 4
SSD_CHUNK = 128
SSD_XBC = SSD_INNER + 2 * SSD_GROUPS * SSD_STATE
DT_PAD = 128

MOBA_HEADS = 8
MOBA_HEAD_DIM = 128
MOBA_INNER = 1024
MOBA_BLOCK = 256
MOBA_TOPK = 3
REL_BUCKETS = 32
REL_MAX_DIST = 1024
MOBA_NEAR = 5
MOBA_QBLOCKS = 2
MOBA_CHAIN_BATCH = 2
MOBA_CHAIN_HEADS = 2
LOG2E = math.log2(math.e)

RWKV_HEADS = 16
RWKV_HEAD_DIM = 64
RWKV_INNER = 1024
RWKV_DECAY_LORA = 64
RWKV_ICLR_LORA = 64
RWKV_GATE_LORA = 128
RWKV_IN = 3 * RWKV_INNER + RWKV_DECAY_LORA + RWKV_ICLR_LORA + RWKV_GATE_LORA
RWKV_LN_EPS = 64e-5
RWKV_CHUNK = 64
RWKV_GROUP = 4

PEER_HEADS = 8
PEER_KEYS = 128
PEER_TOPK = 16
PEER_KEY_DIM = 256

IN_SSD = SSD_INNER + SSD_XBC + SSD_HEADS
IN_MOBA = 3 * MOBA_INNER


def _dot(a, b, precision=None):
    return jnp.dot(a, b, preferred_element_type=F32, precision=precision)


def _dot_nt(a, b, precision=None):
    return lax.dot_general(a, b, (((1,), (1,)), ((), ())), preferred_element_type=F32, precision=precision)


def _dot_tn(a, b, precision=None):
    return lax.dot_general(a, b, (((0,), (0,)), ((), ())), preferred_element_type=F32, precision=precision)


def _split(x):
    hi = x.astype(BF16)
    return hi, (x - hi.astype(F32)).astype(BF16)


def _dot_fx(x, w):
    hi, lo = _split(x)
    return _dot(hi, w) + _dot(lo, w)


def _dot_xf(w, x):
    hi, lo = _split(x)
    return _dot(w, hi) + _dot(w, lo)


def _dot_ff(x, w, nt=False):
    dot = _dot_nt if nt else _dot
    xh, xl = _split(x)
    wh, wl = _split(w)
    return dot(xh, wh) + dot(xl, wh) + dot(xh, wl)


def _rms(x, g):
    return x * lax.rsqrt(jnp.mean(x * x, axis=-1, keepdims=True) + NORM_EPS) * g


def _silu(x):
    return x * jax.nn.sigmoid(x)


def _params(*sem):
    return pltpu.CompilerParams(dimension_semantics=sem, vmem_limit_bytes=VMEM_LIMIT)


def _norm_matmul_kernel(x_ref, g_ref, w_ref, o_ref, *, col_chunk):
    hb = _rms(x_ref[...], g_ref[...]).astype(BF16)
    for c in range(0, o_ref.shape[1], col_chunk):
        o_ref[:, c:c + col_chunk] = _dot(hb, w_ref[:, c:c + col_chunk])


def norm_matmul(x, g, w, tm=512, col_chunk=512):
    t, d = x.shape
    n = w.shape[1]
    col_chunk = math.gcd(n, col_chunk)
    return pl.pallas_call(
        functools.partial(_norm_matmul_kernel, col_chunk=col_chunk),
        grid=(t // tm,),
        in_specs=[pl.BlockSpec((tm, d), lambda i: (i, 0)),
                  pl.BlockSpec((1, d), lambda i: (0, 0)),
                  pl.BlockSpec((d, n), lambda i: (0, 0))],
        out_specs=pl.BlockSpec((tm, n), lambda i: (i, 0)),
        out_shape=jax.ShapeDtypeStruct((t, n), F32),
        compiler_params=_params("parallel"),
        name="norm_matmul",
    )(x, g.reshape(1, d), w)


def _merge_kernel(x_ref, ya_ref, yb_ref, yc_ref, g_ref, wg_ref, bg_ref, pa_ref, pb_ref, pc_ref, wo_ref, o_ref):
    x = x_ref[...]
    hb = _rms(x, g_ref[...]).astype(BF16)
    d = x.shape[1]
    merged = None
    for i, (y_ref, p_ref) in enumerate(((ya_ref, pa_ref), (yb_ref, pb_ref), (yc_ref, pc_ref))):
        gate = jax.nn.sigmoid(_dot(hb, wg_ref[:, i * d:(i + 1) * d]) + bg_ref[:, i * d:(i + 1) * d])
        term = gate * _dot(y_ref[...], p_ref[...])
        merged = term if merged is None else merged + term
    o_ref[...] = x + _dot(merged.astype(BF16), wo_ref[...])


def merge(x, ya, yb, yc, g, w_gate, b_gate, pa, pb, pc, w_out, tm=256):
    t, d = x.shape
    row = lambda i: (i, 0)
    fixed = lambda i: (0, 0)
    return pl.pallas_call(
        _merge_kernel,
        grid=(t // tm,),
        in_specs=[pl.BlockSpec((tm, d), row), pl.BlockSpec((tm, d), row), pl.BlockSpec((tm, d), row),
                  pl.BlockSpec((tm, d), row), pl.BlockSpec((1, d), fixed),
                  pl.BlockSpec((d, 3 * d), fixed), pl.BlockSpec((1, 3 * d), fixed),
                  pl.BlockSpec((d, d), fixed), pl.BlockSpec((d, d), fixed), pl.BlockSpec((d, d), fixed),
                  pl.BlockSpec((d, d), fixed)],
        out_specs=pl.BlockSpec((tm, d), row),
        out_shape=jax.ShapeDtypeStruct((t, d), F32),
        compiler_params=_params("parallel"),
        name="merge",
    )(x, ya, yb, yc, g.reshape(1, d), w_gate, b_gate.reshape(1, 3 * d), pa, pb, pc, w_out)


def _ssd_kernel(xbc_ref, z_ref, dt_ref, cw_ref, cb_ref, dtb_ref, a_ref, dsk_ref, ng_ref, rexp_ref, tri_ref,
                o_ref, buf_ref, state_ref):
    lc = SSD_CHUNK
    c = pl.program_id(1)

    @pl.when(c == 0)
    def _():
        buf_ref[0:8, :] = jnp.zeros((8, SSD_XBC), F32)
        state_ref[...] = jnp.zeros_like(state_ref)

    cur = xbc_ref[...]
    buf_ref[8:8 + lc, :] = cur
    acc = cb_ref[...] + cw_ref[0:1, :] * buf_ref[5:5 + lc, :]
    for k in range(1, SSD_CONV):
        acc = acc + cw_ref[k:k + 1, :] * buf_ref[5 + k:5 + k + lc, :]
    buf_ref[0:8, :] = cur[lc - 8:lc, :]
    xbc = _silu(acc)
    xs = xbc[:, :SSD_INNER]

    dt = jax.nn.softplus(dt_ref[...] + dtb_ref[...])
    acs = _dot_xf(tri_ref[...], dt * a_ref[...])
    acs_t = acs.T
    rexp = rexp_ref[...]
    dt_x = _dot_fx(dt, rexp)
    e_x = _dot_fx(jnp.exp(acs), rexp)
    dte_x = _dot_fx(jnp.exp(acs[lc - 1:lc, :] - acs), rexp)
    xdt = xs * dt_x
    xdt_b = xdt.astype(BF16)
    xw_b = (xdt * dte_x).astype(BF16)
    chunk_decay = e_x[lc - 1:lc, :]

    row = lax.broadcasted_iota(jnp.int32, (lc, lc), 0)
    col = lax.broadcasted_iota(jnp.int32, (lc, lc), 1)
    causal = row >= col
    gw = SSD_INNER // SSD_GROUPS
    hpg = SSD_HEADS // SSD_GROUPS
    n = SSD_STATE
    ys = []
    for g in range(SSD_GROUPS):
        b_g = xbc[:, SSD_INNER + g * n:SSD_INNER + (g + 1) * n].astype(BF16)
        c_g = xbc[:, SSD_INNER + (SSD_GROUPS + g) * n:SSD_INNER + (SSD_GROUPS + g + 1) * n].astype(BF16)
        cb = _dot_nt(c_g, b_g)
        s_in = state_ref[:, g * gw:(g + 1) * gw]
        y_off = _dot(c_g, s_in.astype(BF16)) * e_x[:, g * gw:(g + 1) * gw]
        state_ref[:, g * gw:(g + 1) * gw] = (chunk_decay[:, g * gw:(g + 1) * gw] * s_in
                                             + _dot_tn(b_g, xw_b[:, g * gw:(g + 1) * gw]))
        diag = []
        for e in range(hpg):
            h = g * hpg + e
            seg = jnp.exp(jnp.where(causal, acs[:, h:h + 1] - acs_t[h:h + 1, :], NEG))
            m = (cb * seg).astype(BF16)
            diag.append(_dot(m, xdt_b[:, h * SSD_HEAD_DIM:(h + 1) * SSD_HEAD_DIM]))
        ys.append(jnp.concatenate(diag, axis=-1) + y_off)
    y = jnp.concatenate(ys, axis=-1) + xs * dsk_ref[...]
    y = y * _silu(z_ref[...])
    for g in range(SSD_GROUPS):
        yg = y[:, g * gw:(g + 1) * gw]
        yg = yg * lax.rsqrt(jnp.mean(yg * yg, axis=-1, keepdims=True) + NORM_EPS)
        o_ref[:, g * gw:(g + 1) * gw] = (yg * ng_ref[:, g * gw:(g + 1) * gw]).astype(o_ref.dtype)


def ssd(proj, bsz, slen, conv_w, conv_b, dt_bias, a_log, d_skip, norm_g):
    lc = SSD_CHUNK
    nc = slen // lc
    t = bsz * slen
    pad = DT_PAD - SSD_HEADS
    dtb = jnp.pad(dt_bias.astype(F32), (0, pad)).reshape(1, DT_PAD)
    a = jnp.pad(-jnp.exp(a_log.astype(F32)), (0, pad)).reshape(1, DT_PAD)
    dsk = jnp.repeat(d_skip.astype(F32), SSD_HEAD_DIM).reshape(1, SSD_INNER)
    rexp = jnp.asarray((np.arange(DT_PAD)[:, None] == np.arange(SSD_INNER)[None, :] // SSD_HEAD_DIM)
                       .astype(np.float32), dtype=BF16)
    tri = jnp.asarray(np.tril(np.ones((lc, lc), np.float32)), dtype=BF16)
    rowmap = lambda b, c: (b * nc + c, 0)
    fixed = lambda b, c: (0, 0)
    zblk = SSD_XBC // SSD_INNER
    dtblk = (SSD_XBC + SSD_INNER) // DT_PAD
    return pl.pallas_call(
        _ssd_kernel,
        grid=(bsz, nc),
        in_specs=[pl.BlockSpec((lc, SSD_XBC), rowmap),
                  pl.BlockSpec((lc, SSD_INNER), lambda b, c: (b * nc + c, zblk)),
                  pl.BlockSpec((lc, DT_PAD), lambda b, c: (b * nc + c, dtblk)),
                  pl.BlockSpec((SSD_CONV, SSD_XBC), fixed), pl.BlockSpec((1, SSD_XBC), fixed),
                  pl.BlockSpec((1, DT_PAD), fixed), pl.BlockSpec((1, DT_PAD), fixed),
                  pl.BlockSpec((1, SSD_INNER), fixed), pl.BlockSpec((1, SSD_INNER), fixed),
                  pl.BlockSpec((DT_PAD, SSD_INNER), fixed), pl.BlockSpec((lc, lc), fixed)],
        out_specs=pl.BlockSpec((lc, SSD_INNER), rowmap),
        out_shape=jax.ShapeDtypeStruct((t, SSD_INNER), BF16),
        scratch_shapes=[pltpu.VMEM((lc + 8, SSD_XBC), F32), pltpu.VMEM((SSD_STATE, SSD_INNER), F32)],
        compiler_params=_params("parallel", "arbitrary"),
        name="ssd",
    )(proj, proj, proj, conv_w.astype(F32), conv_b.astype(F32).reshape(1, SSD_XBC), dtb, a, dsk,
      norm_g.astype(F32).reshape(1, SSD_INNER), rexp, tri)


def _ssd_weight(w_in_l):
    w = w_in_l[:, :IN_SSD]
    z, xbc, dt = w[:, :SSD_INNER], w[:, SSD_INNER:SSD_INNER + SSD_XBC], w[:, SSD_INNER + SSD_XBC:]
    dt = jnp.pad(dt, ((0, 0), (0, DT_PAD - SSD_HEADS)))
    return jnp.concatenate([xbc, z, dt], axis=1).astype(BF16)


def _t5_bucket(dist):
    max_exact = REL_BUCKETS // 2
    d = jnp.maximum(dist, 0)
    df = jnp.maximum(d, 1).astype(F32)
    large = max_exact + (jnp.log(df / max_exact) / math.log(REL_MAX_DIST / max_exact)
                         * (REL_BUCKETS - max_exact)).astype(jnp.int32)
    large = jnp.minimum(large, REL_BUCKETS - 1)
    return jnp.where(d < max_exact, d, large)


def _bias_table_kernel(bias_ref, bucket_ref, o_ref):
    h = pl.program_id(0)
    for d in range(MOBA_NEAR):
        bk = bucket_ref[d]
        acc = jnp.where(bk < 0, NEG, 0.0)
        for b in range(REL_BUCKETS):
            acc = jnp.where(bk == b, bias_ref[h, b] * LOG2E, acc)
        o_ref[d] = acc
    o_ref[MOBA_NEAR] = jnp.zeros(o_ref.shape[1:], F32)


def moba_bias_table(rel_bias):
    blk = MOBA_BLOCK
    q = jnp.arange(blk, dtype=jnp.int32)[None, :, None]
    k = jnp.arange(blk, dtype=jnp.int32)[None, None, :]
    dist = jnp.arange(MOBA_NEAR, dtype=jnp.int32)[:, None, None] * blk + q - k
    bucket = jnp.where(dist < 0, -1, _t5_bucket(dist))
    return pl.pallas_call(
        _bias_table_kernel,
        grid=(MOBA_HEADS,),
        in_specs=[pl.BlockSpec(memory_space=pltpu.SMEM),
                  pl.BlockSpec((MOBA_NEAR, blk, blk), lambda h: (0, 0, 0))],
        out_specs=pl.BlockSpec((None, MOBA_NEAR + 1, blk, blk), lambda h: (h, 0, 0, 0)),
        out_shape=jax.ShapeDtypeStruct((MOBA_HEADS, MOBA_NEAR + 1, blk, blk), F32),
        compiler_params=_params("arbitrary"),
        name="moba_bias_table",
    )(rel_bias.astype(F32).T, bucket)


def _moba_prep_kernel(q_ref, k_ref, v_ref, qg_ref, kg_ref, far_ref, qa_ref, kb_ref, vb_ref, kmean_ref):
    i = pl.program_id(2)
    blk, dh = q_ref.shape
    half = dh // 2

    @pl.when(i == 0)
    def _():
        kmean_ref[...] = jnp.zeros_like(kmean_ref)

    qn = _rms(q_ref[...], qg_ref[...])
    kn = _rms(k_ref[...], kg_ref[...])
    gate = _dot_nt(qn, kmean_ref[...], HIGHEST)
    col = lax.broadcasted_iota(jnp.int32, (blk, dh), 1)
    colf = col.astype(F32)
    valid = col < i
    g = jnp.where(valid, gate, NEG)
    sel = col < 0
    for _ in range(MOBA_TOPK):
        m = jnp.max(g, axis=-1, keepdims=True)
        idx = jnp.min(jnp.where(g == m, colf, float(dh)), axis=-1, keepdims=True)
        hit = colf == idx
        sel = sel | hit
        g = jnp.where(hit, -jnp.inf, g)
    sel = sel & valid

    far = far_ref[...] * LOG2E
    far_hi = far.astype(BF16).astype(F32)
    far_lo = far - far_hi
    sel_f = jnp.where(sel, 1.0, 0.0)
    low = col >= half
    sel2 = jnp.where(low, pltpu.roll(sel_f, half, 1), sel_f) > 0.5
    jcol = jnp.where(low, col - half, col)
    is_far = (i - jcol) >= MOBA_NEAR
    pen_hi = jnp.where(jcol == i, 0.0, jnp.where(sel2, jnp.where(is_far, far_hi, 0.0), NEG))
    pen_lo = jnp.where(sel2 & is_far, far_lo, 0.0)
    qa_ref[:, 0:dh] = (qn * (dh ** -0.5 * LOG2E)).astype(BF16)
    qa_ref[:, dh:2 * dh] = jnp.where(low, pen_lo, pen_hi).astype(BF16)
    kb_ref[...] = kn.astype(BF16)
    vb_ref[...] = v_ref[...].astype(BF16)
    kmean_ref[pl.ds(i, 1), :] = jnp.mean(kn, axis=0, keepdims=True)


def _moba_attn_kernel(qa_ref, kb_ref, vb_ref, tab_ref, o_ref, m_ref, acc_ref):
    a = pl.program_id(2)
    blk = MOBA_BLOCK
    dh = m_ref.shape[-1]
    chains = [(bi, hi) for bi in range(qa_ref.shape[0]) for hi in range(qa_ref.shape[1])]
    q = [qa_ref[bi, hi] for bi, hi in chains]
    m_ref[...] = jnp.full_like(m_ref, 0.1 * NEG)
    acc_ref[...] = jnp.zeros_like(acc_ref)

    def step(off, nk, tabs):
        row = lax.broadcasted_iota(jnp.int32, (nk, dh), 0)
        lane = lax.broadcasted_iota(jnp.int32, (nk, dh), 1)
        key_block = off // blk + lax.shift_right_logical(row, blk.bit_length() - 1)
        onehot = jnp.where((lane & (dh // 2 - 1)) == key_block, 1.0, 0.0).astype(BF16)
        ones = jnp.ones((nk, dh), BF16)
        s = [_dot_nt(q[n], jnp.concatenate([kb_ref[bi, hi, pl.ds(off, nk), :], onehot], axis=1))
             for n, (bi, hi) in enumerate(chains)]
        if tabs is not None:
            s = [x + tabs[hi] for x, (_, hi) in zip(s, chains)]
        m_old = [m_ref[n] for n in range(len(chains))]
        m_new = [jnp.maximum(mo, jnp.max(x, axis=-1, keepdims=True)) for mo, x in zip(m_old, s)]
        alpha = [jnp.exp2(mo - mn) for mo, mn in zip(m_old, m_new)]
        p = [jnp.exp2(x - jnp.concatenate([mn] * (nk // dh), axis=-1)).astype(BF16) for x, mn in zip(s, m_new)]
        pv = [_dot(p[n], jnp.concatenate([vb_ref[bi, hi, pl.ds(off, nk), :], ones], axis=1))
              for n, (bi, hi) in enumerate(chains)]
        for n in range(len(chains)):
            acc_ref[n] = jnp.concatenate([alpha[n], alpha[n]], axis=-1) * acc_ref[n] + pv[n]
            m_ref[n] = m_new[n]

    first = MOBA_QBLOCKS * a
    n_far = jnp.maximum(first - (MOBA_NEAR - 1), 0)

    def far_body(t, c):
        step(pl.multiple_of(t * (MOBA_QBLOCKS * blk), MOBA_QBLOCKS * blk), MOBA_QBLOCKS * blk, None)
        return c

    def near_body(u, c):
        j0 = n_far + MOBA_QBLOCKS * u
        tabs = [jnp.concatenate(
            [jnp.concatenate([tab_ref[hi, jnp.clip(first + r - (j0 + kb), 0, MOBA_NEAR)]
                              for kb in range(MOBA_QBLOCKS)], axis=1) for r in range(MOBA_QBLOCKS)], axis=0)
            for hi in range(qa_ref.shape[1])]
        step(pl.multiple_of(j0 * blk, MOBA_QBLOCKS * blk), MOBA_QBLOCKS * blk, tabs)
        return c

    lax.fori_loop(0, n_far // MOBA_QBLOCKS, far_body, 0)
    lax.fori_loop(0, (first + MOBA_QBLOCKS - n_far) // MOBA_QBLOCKS, near_body, 0)
    for n, (bi, hi) in enumerate(chains):
        acc = acc_ref[n]
        o_ref[bi, :, hi * dh:(hi + 1) * dh] = (acc[:, :dh] / acc[:, dh:]).astype(o_ref.dtype)


def moba(proj, bsz, slen, q_norm_g, k_norm_g, bias_table, far_bias):
    blk, dh, nh = MOBA_BLOCK, MOBA_HEAD_DIM, MOBA_HEADS
    nb = slen // blk
    tq = MOBA_QBLOCKS * blk
    assert slen % tq == 0 and nb <= dh // 2 and (MOBA_NEAR - 1) % MOBA_QBLOCKS == 0
    t = bsz * slen
    far = jnp.broadcast_to(far_bias.astype(F32)[:, None, None], (nh, 1, dh))
    per_block = lambda width: pl.BlockSpec((None, None, blk, width), lambda b, h, i: (b, h, i, 0))
    qa, kb, vb = pl.pallas_call(
        _moba_prep_kernel,
        grid=(bsz, nh, nb),
        in_specs=[pl.BlockSpec((blk, dh), lambda b, h, i: (b * nb + i, h)),
                  pl.BlockSpec((blk, dh), lambda b, h, i: (b * nb + i, nh + h)),
                  pl.BlockSpec((blk, dh), lambda b, h, i: (b * nb + i, 2 * nh + h)),
                  pl.BlockSpec((1, dh), lambda b, h, i: (0, 0)),
                  pl.BlockSpec((1, dh), lambda b, h, i: (0, 0)),
                  pl.BlockSpec((None, 1, dh), lambda b, h, i: (h, 0, 0))],
        out_specs=[per_block(2 * dh), per_block(dh), per_block(dh)],
        out_shape=[jax.ShapeDtypeStruct((bsz, nh, slen, 2 * dh), BF16),
                   jax.ShapeDtypeStruct((bsz, nh, slen, dh), BF16),
                   jax.ShapeDtypeStruct((bsz, nh, slen, dh), BF16)],
        scratch_shapes=[pltpu.VMEM((dh, dh), F32)],
        compiler_params=_params("parallel", "parallel", "arbitrary"),
        name="moba_prep",
    )(proj, proj, proj, q_norm_g.astype(F32).reshape(1, dh), k_norm_g.astype(F32).reshape(1, dh), far)
    cb, chd = MOBA_CHAIN_BATCH, MOBA_CHAIN_HEADS
    assert bsz % cb == 0 and nh % chd == 0
    whole = pl.BlockSpec((cb, chd, slen, dh), lambda b, h, a: (b, h, 0, 0), pipeline_mode=pl.Buffered(1))
    out = pl.pallas_call(
        _moba_attn_kernel,
        grid=(bsz // cb, nh // chd, slen // tq),
        in_specs=[pl.BlockSpec((cb, chd, tq, 2 * dh), lambda b, h, a: (b, h, a, 0)), whole, whole,
                  pl.BlockSpec((chd, MOBA_NEAR + 1, blk, blk), lambda b, h, a: (h, 0, 0, 0))],
        out_specs=pl.BlockSpec((cb, tq, chd * dh), lambda b, h, a: (b, a, h)),
        out_shape=jax.ShapeDtypeStruct((bsz, slen, nh * dh), BF16),
        scratch_shapes=[pltpu.VMEM((cb * chd, tq, dh), F32), pltpu.VMEM((cb * chd, tq, 2 * dh), F32)],
        compiler_params=_params("parallel", "parallel", "arbitrary"),
        name="moba_attn",
    )(qa, kb, vb, bias_table)
    return out.reshape(t, nh * dh)


def _rwkv_prep_kernel(cur_ref, prev_ref, mu_ref, w0_ref, ww2_ref, a0_ref, wa2_ref, wg2_ref, kk_ref, ka_ref,
                      hsum_ref, hexp_ref, r_ref, lw_ref, k_ref, v_ref, a_ref, b_ref, g_ref, buf_ref,
                      *, tiles_per_seq):
    i = pl.program_id(0)
    tm = cur_ref.shape[0]
    c = RWKV_INNER
    cur = cur_ref[...]
    first = (i % tiles_per_seq) == 0
    buf_ref[8:8 + tm, :] = cur
    buf_ref[0:8, :] = jnp.where(first, 0.0, prev_ref[...])
    prev = buf_ref[7:7 + tm, :]
    xs = cur + (prev - cur) * mu_ref[...]
    r, k, v = xs[:, :c], xs[:, c:2 * c], xs[:, 2 * c:3 * c]
    o1 = 3 * c + RWKV_DECAY_LORA
    o2 = o1 + RWKV_ICLR_LORA
    wl, al, gl = xs[:, 3 * c:o1], xs[:, o1:o2], xs[:, o2:]
    w = -jax.nn.softplus(-(w0_ref[...] + _dot_ff(jnp.tanh(wl), ww2_ref[...]))) - 0.5
    a = jax.nn.sigmoid(a0_ref[...] + _dot_ff(al, wa2_ref[...]))
    kk = k * kk_ref[...]
    ss = _dot_fx(_dot_fx(kk * kk, hsum_ref[...]), hexp_ref[...])
    kk = kk * lax.rsqrt(jnp.maximum(ss, 1e-12))
    r_ref[...] = r
    lw_ref[...] = -jnp.exp(w)
    k_ref[...] = k * (1.0 + (a - 1.0) * ka_ref[...])
    v_ref[...] = v
    a_ref[...] = -kk
    b_ref[...] = kk * a
    g_ref[...] = _dot_ff(jax.nn.sigmoid(gl), wg2_ref[...])


def _bd_rows(y, mask):
    return jnp.where(mask, jnp.concatenate([y] * RWKV_GROUP, axis=0), 0.0)


def _rwkv_scan_kernel(r_ref, lw_ref, k_ref, v_ref, a_ref, b_ref, g_ref, rk_ref, lng_ref, lnb_ref,
                      tri_ref, lows_ref, lowi_ref, eye_ref, bd_ref, avg_ref, o_ref, s_ref, *, mxu_dtype):
    ci = pl.program_id(1)
    ch = RWKV_CHUNK

    @pl.when(ci == 0)
    def _():
        s_ref[...] = jnp.zeros_like(s_ref)

    prec = HIGHEST if mxu_dtype == F32 else None

    def mm(x, y):
        return _dot(x.astype(mxu_dtype), y.astype(mxu_dtype), prec)

    def mm_nt(x, y):
        return _dot_nt(x.astype(mxu_dtype), y.astype(mxu_dtype), prec)

    def mm_tn(x, y):
        return _dot_tn(x.astype(mxu_dtype), y.astype(mxu_dtype), prec)

    bd = bd_ref[...] > 0.5
    strict = lows_ref[...] > 0.5
    incl = lowi_ref[...] > 0.5
    gw = bd_ref.shape[0]
    eye_big = lax.broadcasted_iota(jnp.int32, (gw, gw), 0) == lax.broadcasted_iota(jnp.int32, (gw, gw), 1)
    avg = avg_ref[...]

    bsz = r_ref.shape[0]
    chains = [(bi, gi) for bi in range(bsz) for gi in range(r_ref.shape[2] // gw)]
    cols = [slice(gi * gw, (gi + 1) * gw) for _, gi in chains]

    def each(fn, *lists):
        return [fn(*xs) for xs in zip(*lists)]

    def bdr(ys):
        return [_bd_rows(y, bd) for y in ys]

    r, lw, k, v, a, b = ([ref[bi, :, c] for (bi, _), c in zip(chains, cols)]
                         for ref in (r_ref, lw_ref, k_ref, v_ref, a_ref, b_ref))
    cl = each(lambda x: _dot_xf(tri_ref[...], x), lw)
    cl_end = [x[ch - 1:ch, :] for x in cl]
    e_neg = [jnp.exp(-x) for x in cl]
    a_t = each(lambda x, c, l: x * jnp.exp(c - l), a, cl, lw)
    r_t = each(lambda x, c: x * jnp.exp(c), r, cl)
    b_t = each(jnp.multiply, b, e_neg)
    k_t = each(jnp.multiply, k, e_neg)
    e_end = each(lambda ce, c: jnp.exp(ce - c), cl_end, cl)
    b_h = each(jnp.multiply, b, e_end)
    k_h = each(jnp.multiply, k, e_end)

    ar = each(lambda x, y: jnp.concatenate([x, y], axis=0), a_t, r_t)
    gb = each(mm_nt, ar, bdr(b_t))
    gk = each(mm_nt, ar, bdr(k_t))
    a_ab = [jnp.where(strict, x[:ch], 0.0) for x in gb]
    a_ak = [jnp.where(strict, x[:ch], 0.0) for x in gk]
    q_rb = [jnp.where(incl, x[ch:], 0.0) for x in gb]
    q_rk = [jnp.where(incl, x[ch:], 0.0) for x in gk]

    t_inv = [eye_ref[...] + x for x in a_ab]
    p = a_ab
    for _ in range(int(math.log2(ch)) - 1):
        p = each(mm, p, bdr(p))
        t_inv = each(lambda t, tp: t + tp, t_inv, each(mm, t_inv, bdr(p)))

    vbd = bdr(v)
    w_t = each(mm, t_inv, bdr(a_t))
    u0 = each(mm, t_inv, bdr(each(mm, a_ak, vbd)))
    z = each(lambda x, y: x + y, r_t, each(mm, q_rb, bdr(w_t)))
    y_loc = each(lambda x, y: x + y, each(mm, q_rb, bdr(u0)), each(mm, q_rk, vbd))
    bwu = each(lambda x, w, u: mm_tn(x, jnp.concatenate([w, u], axis=1)), b_h, w_t, u0)
    kv = each(mm_tn, k_h, v)
    m_s = each(lambda x, ce: jnp.where(bd, x[:, :gw], 0.0) + jnp.where(eye_big, jnp.exp(ce), 0.0), bwu, cl_end)
    n_s = each(lambda x, y: jnp.where(bd, x[:, gw:] + y, 0.0), bwu, kv)
    s0 = [s_ref[n] for n in range(len(chains))]
    zs = each(lambda x, m, s: mm(jnp.concatenate([x, m], axis=0), s), z, m_s, s0)
    y = each(lambda x, yl: x[:ch] + yl, zs, y_loc)
    for n in range(len(chains)):
        s_ref[n] = zs[n][ch:] + n_s[n]

    mean = [_dot_fx(x, avg) for x in y]
    dlt = each(lambda x, m: x - m, y, mean)
    var = [_dot_fx(x * x, avg) for x in dlt]
    rkk = each(lambda x, kk, c: _dot_fx(x * kk * rk_ref[:, c], avg), r, k, cols)
    for n, ((bi, _), c) in enumerate(zip(chains, cols)):
        yn = dlt[n] * lax.rsqrt(var[n] + RWKV_LN_EPS) * lng_ref[:, c] + lnb_ref[:, c]
        bonus = rkk[n] * float(RWKV_HEAD_DIM) * v[n]
        o_ref[bi, :, c] = ((yn + bonus) * g_ref[bi, :, c]).astype(o_ref.dtype)


def rwkv(proj, bsz, slen, mu, w0, w_w2, a0, w_a2, w_g2, k_k, k_a, r_k, lnx_g, lnx_b, tm=256, mxu_dtype=BF16,
         groups_per_step=4):
    c, ch, grp, n = RWKV_INNER, RWKV_CHUNK, RWKV_GROUP, RWKV_HEAD_DIM
    t = bsz * slen
    f = lambda p: p.astype(F32).reshape(1, -1)
    heads = np.arange(c) // n
    hsum = jnp.asarray((heads[:, None] == np.arange(128)[None, :]).astype(np.float32), dtype=BF16)
    hexp = hsum.T
    row = lambda i: (i, 0)
    fixed = lambda i: (0, 0)
    wide = pl.BlockSpec((tm, c), row)
    vec = pl.BlockSpec((1, c), fixed)
    r, lw, k, v, a, b, g = pl.pallas_call(
        functools.partial(_rwkv_prep_kernel, tiles_per_seq=slen // tm),
        grid=(t // tm,),
        in_specs=[pl.BlockSpec((tm, RWKV_IN), row),
                  pl.BlockSpec((8, RWKV_IN), lambda i: (jnp.maximum(i * (tm // 8) - 1, 0), 0)),
                  pl.BlockSpec((1, RWKV_IN), fixed), vec,
                  pl.BlockSpec((RWKV_DECAY_LORA, c), fixed), vec,
                  pl.BlockSpec((RWKV_ICLR_LORA, c), fixed), pl.BlockSpec((RWKV_GATE_LORA, c), fixed),
                  vec, vec, pl.BlockSpec((c, 128), fixed), pl.BlockSpec((128, c), fixed)],
        out_specs=[wide] * 7,
        out_shape=[jax.ShapeDtypeStruct((t, c), F32)] * 7,
        scratch_shapes=[pltpu.VMEM((tm + 8, RWKV_IN), F32)],
        compiler_params=_params("parallel"),
        name="rwkv_prep",
    )(proj, proj, f(mu), f(w0), w_w2.astype(F32), f(a0), w_a2.astype(F32), w_g2.astype(F32), f(k_k), f(k_a),
      hsum, hexp)

    gw = grp * n
    nc = slen // ch
    tt = np.arange(ch)
    colt = np.arange(gw) % ch
    hrow = np.arange(gw) // n
    tri = jnp.asarray((tt[:, None] >= tt[None, :]).astype(np.float32), dtype=BF16)
    lows = jnp.asarray((colt[None, :] < tt[:, None]).astype(np.float32))
    lowi = jnp.asarray((colt[None, :] <= tt[:, None]).astype(np.float32))
    eye = jnp.asarray((colt[None, :] == tt[:, None]).astype(np.float32))
    bdm = (hrow[:, None] == hrow[None, :]).astype(np.float32)
    bd = jnp.asarray(bdm)
    avg = jnp.asarray(bdm / n, dtype=BF16)
    sw = groups_per_step * gw
    blk = pl.BlockSpec((bsz, ch, sw), lambda gi, ci: (0, ci, gi))
    gvec = pl.BlockSpec((1, sw), lambda gi, ci: (0, gi))
    const = lambda shape: pl.BlockSpec(shape, lambda gi, ci: (0, 0))
    seq = lambda x: x.reshape(bsz, slen, c)
    out = pl.pallas_call(
        functools.partial(_rwkv_scan_kernel, mxu_dtype=mxu_dtype),
        grid=(c // sw, nc),
        in_specs=[blk] * 7 + [gvec] * 3 + [const((ch, ch)), const((ch, gw)), const((ch, gw)), const((ch, gw)),
                                           const((gw, gw)), const((gw, gw))],
        out_specs=blk,
        out_shape=jax.ShapeDtypeStruct((bsz, slen, c), BF16),
        scratch_shapes=[pltpu.VMEM((bsz * groups_per_step, gw, gw), F32)],
        compiler_params=_params("parallel", "arbitrary"),
        name="rwkv_scan",
    )(seq(r), seq(lw), seq(k), seq(v), seq(a), seq(b), seq(g), f(r_k), f(lnx_g), f(lnx_b),
      tri, lows, lowi, eye, bd, avg)
    return out.reshape(t, c)


PEER_RANKS = PEER_TOPK
PEER_PAIRS = [(p, q) for p in range(PEER_RANKS) for q in range(PEER_RANKS // (p + 1))]
PEER_CAND_ROWS = -(-len(PEER_PAIRS) // 8) * 8
PEER_TOP_ROWS = -(-2 * PEER_RANKS // 8) * 8


def _peer_kernel(x_ref, g_ref, wq_ref, k1_ref, k2_ref, u_ref, vt_ref, o_ref,
                 h2_ref, cnt_ref, e1_ref, r2_ref, e2_ref, acc_ref, top_ref, cand_ref, sel_ref,
                 hu0_ref, hu1_ref, p0_ref, p1_ref):
    e = pl.program_id(1)
    n_blocks = 2 * (pl.num_programs(1) - 1)
    nk = PEER_KEYS
    half = PEER_KEY_DIM // 2
    eb = u_ref.shape[0] // 2

    @pl.when(e == 0)
    def _():
        for ref in (hu0_ref, hu1_ref, p0_ref, p1_ref):
            ref[...] = jnp.zeros_like(ref)
        hb = _rms(x_ref[...], g_ref[...]).astype(BF16)
        h2_ref[...] = hb
        q = _dot(hb, wq_ref[...])
        cand_ref[...] = jnp.full_like(cand_ref, -jnp.inf)
        for h in range(PEER_HEADS):
            base = h * PEER_KEY_DIM
            s1 = _dot_ff(k1_ref[...], q[:, base:base + half], nt=True)
            s2 = _dot_ff(k2_ref[...], q[:, base + half:base + 2 * half], nt=True)
            ranks = []
            for side, s in enumerate((s1, s2)):
                cur = s
                rank = jnp.full(s.shape, float(PEER_KEYS), F32)
                for rnk in range(PEER_RANKS):
                    m = jnp.max(cur, axis=0, keepdims=True)
                    top_ref[side * PEER_RANKS + rnk:side * PEER_RANKS + rnk + 1, :] = m
                    hit = cur == m
                    rank = jnp.where(hit, float(rnk), rank)
                    cur = jnp.where(hit, -jnp.inf, cur)
                ranks.append(rank)
            for ci, (p, qq) in enumerate(PEER_PAIRS):
                cand_ref[ci:ci + 1, :] = top_ref[p:p + 1, :] + top_ref[PEER_RANKS + qq:PEER_RANKS + qq + 1, :]
            cand = cand_ref[...]
            cur = cand
            best = jnp.max(cur, axis=0, keepdims=True)
            tau = best
            for _ in range(PEER_TOPK - 1):
                cur = jnp.where(cur == tau, -jnp.inf, cur)
                tau = jnp.max(cur, axis=0, keepdims=True)
            chosen = cand >= tau
            zsum = jnp.sum(jnp.where(chosen, jnp.exp(cand - best), 0.0), axis=0, keepdims=True)
            sel_ref[...] = jnp.where(chosen, 1.0, 0.0)
            rank1 = ranks[0].astype(BF16)
            cnt = jnp.zeros(s1.shape, BF16)
            row0 = 0
            for p in range(PEER_RANKS):
                nq = PEER_RANKS // (p + 1)
                n_p = jnp.sum(sel_ref[row0:row0 + nq, :], axis=0, keepdims=True).astype(BF16)
                cnt = jnp.where(rank1 == jnp.asarray(p, BF16), n_p, cnt)
                row0 += nq
            cnt_ref[h] = cnt.astype(F32)
            e1_ref[h] = jnp.exp(s1 - top_ref[0:1, :]) / zsum
            r2_ref[h] = ranks[1].astype(BF16)
            e2_ref[h] = jnp.exp(s2 - top_ref[PEER_RANKS:PEER_RANKS + 1, :]).astype(BF16)
        acc_ref[...] = jnp.zeros_like(acc_ref)

    hu_slots, p_slots = (hu0_ref, hu1_ref), (p0_ref, p1_ref)
    tt = h2_ref.shape[0]
    d = acc_ref.shape[0]
    tok = [slice(n * (tt // 2), (n + 1) * (tt // 2)) for n in range(2)]
    for par in range(2):
        tick = 2 * e + par
        blk_b = jnp.clip(tick - 1, 0, n_blocks - 1)
        p_rd, p_wr, hu_rd, hu_wr = p_slots[par], p_slots[1 - par], hu_slots[1 - par], hu_slots[par]

        def down(n, r):
            rows = slice(r * (d // 2), (r + 1) * (d // 2))
            acc_ref[rows, tok[n]] += _dot(vt_ref[rows, par * eb:(par + 1) * eb], p_rd[:, tok[n]])

        def up(n, r):
            rows = slice(r * (eb // 2), (r + 1) * (eb // 2))
            hu_wr[rows, tok[n]] = _dot_nt(u_ref[par * eb + r * (eb // 2):par * eb + (r + 1) * (eb // 2), :],
                                          h2_ref[tok[n], :])

        def weigh(n, ib):
            rows = slice(ib * nk, (ib + 1) * nk)
            i = blk_b * (eb // nk) + ib
            hu = hu_rd[rows, tok[n]]
            act = (0.5 * hu * (1.0 + lax.erf(hu * (2.0 ** -0.5)))).astype(BF16)
            w = None
            for h in range(PEER_HEADS):
                cnt = cnt_ref[h, pl.ds(i, 1), tok[n]].astype(BF16)
                gate1 = e1_ref[h, pl.ds(i, 1), tok[n]].astype(BF16)
                term = jnp.where(r2_ref[h, :, tok[n]] < cnt, e2_ref[h, :, tok[n]] * gate1, jnp.zeros((), BF16))
                w = term if w is None else w + term
            p_wr[rows, tok[n]] = w * act

        matmul_jobs = [(fn, n, r) for n in range(2) for r in range(2) for fn in (down, up)]
        vector_jobs = [(n, ib) for n in range(2) for ib in range(eb // nk)]
        assert len(matmul_jobs) == len(vector_jobs)
        for (fn, n, r), (vn, ib) in zip(matmul_jobs, vector_jobs):
            fn(n, r)
            weigh(vn, ib)

    @pl.when(e == pl.num_programs(1) - 1)
    def _():
        o_ref[...] = x_ref[...] + acc_ref[...].T


def peer(x, g, wq, k1, k2, u, vt, tt=512, eb=512):
    t, d = x.shape
    ne = u.shape[0]
    nh, nk = PEER_HEADS, PEER_KEYS
    nb2 = ne // (2 * eb)
    assert ne % (2 * eb) == 0
    return pl.pallas_call(
        _peer_kernel,
        grid=(t // tt, nb2 + 1),
        in_specs=[pl.BlockSpec((tt, d), lambda i, e: (i, 0)),
                  pl.BlockSpec((1, d), lambda i, e: (0, 0)),
                  pl.BlockSpec((d, nh * PEER_KEY_DIM), lambda i, e: (0, 0)),
                  pl.BlockSpec((nk, PEER_KEY_DIM // 2), lambda i, e: (0, 0)),
                  pl.BlockSpec((nk, PEER_KEY_DIM // 2), lambda i, e: (0, 0)),
                  pl.BlockSpec((2 * eb, d), lambda i, e: (jnp.minimum(e, nb2 - 1), 0)),
                  pl.BlockSpec((d, 2 * eb), lambda i, e: (0, jnp.maximum(e - 1, 0)))],
        out_specs=pl.BlockSpec((tt, d), lambda i, e: (i, 0)),
        out_shape=jax.ShapeDtypeStruct((t, d), F32),
        scratch_shapes=[pltpu.VMEM((tt, d), BF16),
                        pltpu.VMEM((nh, nk, tt), F32), pltpu.VMEM((nh, nk, tt), F32),
                        pltpu.VMEM((nh, nk, tt), BF16), pltpu.VMEM((nh, nk, tt), BF16),
                        pltpu.VMEM((d, tt), F32),
                        pltpu.VMEM((PEER_TOP_ROWS, tt), F32), pltpu.VMEM((PEER_CAND_ROWS, tt), F32),
                        pltpu.VMEM((PEER_CAND_ROWS, tt), F32),
                        pltpu.VMEM((eb, tt), F32), pltpu.VMEM((eb, tt), F32),
                        pltpu.VMEM((eb, tt), BF16), pltpu.VMEM((eb, tt), BF16)],
        compiler_params=_params("parallel", "arbitrary"),
        name="peer",
    )(x, g.astype(F32).reshape(1, d), wq, k1.astype(F32), k2.astype(F32), u, vt)


def kernel(x, rel_bias, norm1_g, w_in, conv_w, conv_b, dt_bias, a_log, d_skip, ssd_norm_g, q_norm_g, k_norm_g, rwkv_mu, w0, w_w2, a0, w_a2, w_g2, k_k, k_a, r_k, lnx_g, lnx_b, p_ssd, p_moba, p_rwkv, w_gate, b_gate, w_out, norm2_g, peer_wq, peer_k1, peer_k2, peer_u, peer_v):
    bsz, slen, d = x.shape
    t = bsz * slen
    xf = x.astype(F32).reshape(t, d)
    bias_table = moba_bias_table(rel_bias)
    far_bias = rel_bias[REL_BUCKETS - 1]
    for l in range(w_in.shape[0]):
        g1 = norm1_g[l].astype(F32)
        w_moba = w_in[l][:, IN_SSD:IN_SSD + IN_MOBA].astype(BF16)
        w_rwkv = w_in[l][:, IN_SSD + IN_MOBA:].astype(BF16)
        ya = ssd(norm_matmul(xf, g1, _ssd_weight(w_in[l])), bsz, slen, conv_w[l], conv_b[l], dt_bias[l],
                 a_log[l], d_skip[l], ssd_norm_g[l])
        yb = moba(norm_matmul(xf, g1, w_moba), bsz, slen, q_norm_g[l], k_norm_g[l], bias_table, far_bias)
        yc = rwkv(norm_matmul(xf, g1, w_rwkv), bsz, slen, rwkv_mu[l], w0[l], w_w2[l], a0[l], w_a2[l], w_g2[l],
                  k_k[l], k_a[l], r_k[l], lnx_g[l], lnx_b[l])
        xf = merge(xf, ya, yb, yc, g1, w_gate[l].astype(BF16), b_gate[l].astype(F32), p_ssd[l].astype(BF16),
                   p_moba[l].astype(BF16), p_rwkv[l].astype(BF16), w_out[l].astype(BF16))
        xf = peer(xf, norm2_g[l], peer_wq[l].astype(BF16), peer_k1[l], peer_k2[l], peer_u[l].astype(BF16),
                  peer_v[l].astype(BF16).T)
    return xf.reshape(bsz, slen, d).astype(x.dtype)
```

```python
import functools
import math

import numpy as np
import jax
import jax.numpy as jnp
from jax import lax
from jax.experimental import pallas as pl
from jax.experimental.pallas import tpu as pltpu

F32 = jnp.float32
BF16 = jnp.bfloat16
HIGHEST = lax.Precision.HIGHEST

D_MODEL = 1024
NORM_EPS = 1e-6
NEG = -1e30
VMEM_LIMIT = 56 * 1024 * 1024

SSD_HEADS = 16
SSD_HEAD_DIM = 64
SSD_INNER = 1024
SSD_GROUPS = 4
SSD_STATE = 128
SSD_CONV = 4
SSD_CHUNK = 128
SSD_XBC = SSD_INNER + 2 * SSD_GROUPS * SSD_STATE
DT_PAD = 128

MOBA_HEADS = 8
MOBA_HEAD_DIM = 128
MOBA_INNER = 1024
MOBA_BLOCK = 256
MOBA_TOPK = 3
REL_BUCKETS = 32
REL_MAX_DIST = 1024
MOBA_NEAR = 5
MOBA_QBLOCKS = 2
MOBA_CHAIN_BATCH = 2
MOBA_CHAIN_HEADS = 2
LOG2E = math.log2(math.e)

RWKV_HEADS = 16
RWKV_HEAD_DIM = 64
RWKV_INNER = 1024
RWKV_DECAY_LORA = 64
RWKV_ICLR_LORA = 64
RWKV_GATE_LORA = 128
RWKV_IN = 3 * RWKV_INNER + RWKV_DECAY_LORA + RWKV_ICLR_LORA + RWKV_GATE_LORA
RWKV_LN_EPS = 64e-5
RWKV_CHUNK = 64
RWKV_GROUP = 4

PEER_HEADS = 8
PEER_KEYS = 128
PEER_TOPK = 16
PEER_KEY_DIM = 256

IN_SSD = SSD_INNER + SSD_XBC + SSD_HEADS
IN_MOBA = 3 * MOBA_INNER


def _dot(a, b, precision=None):
    return jnp.dot(a, b, preferred_element_type=F32, precision=precision)


def _dot_nt(a, b, precision=None):
    return lax.dot_general(a, b, (((1,), (1,)), ((), ())), preferred_element_type=F32, precision=precision)


def _dot_tn(a, b, precision=None):
    return lax.dot_general(a, b, (((0,), (0,)), ((), ())), preferred_element_type=F32, precision=precision)


def _split(x):
    hi = x.astype(BF16)
    return hi, (x - hi.astype(F32)).astype(BF16)


def _dot_fx(x, w):
    hi, lo = _split(x)
    return _dot(hi, w) + _dot(lo, w)


def _dot_xf(w, x):
    hi, lo = _split(x)
    return _dot(w, hi) + _dot(w, lo)


def _dot_ff(x, w, nt=False):
    dot = _dot_nt if nt else _dot
    xh, xl = _split(x)
    wh, wl = _split(w)
    return dot(xh, wh) + dot(xl, wh) + dot(xh, wl)


def _rms(x, g):
    return x * lax.rsqrt(jnp.mean(x * x, axis=-1, keepdims=True) + NORM_EPS) * g


def _silu(x):
    return x * jax.nn.sigmoid(x)


def _params(*sem):
    return pltpu.CompilerParams(dimension_semantics=sem, vmem_limit_bytes=VMEM_LIMIT)


def _norm_matmul_kernel(x_ref, g_ref, w_ref, o_ref, *, col_chunk):
    hb = _rms(x_ref[...], g_ref[...]).astype(BF16)
    for c in range(0, o_ref.shape[1], col_chunk):
        o_ref[:, c:c + col_chunk] = _dot(hb, w_ref[:, c:c + col_chunk])


def norm_matmul(x, g, w, tm=512, col_chunk=512):
    t, d = x.shape
    n = w.shape[1]
    col_chunk = math.gcd(n, col_chunk)
    return pl.pallas_call(
        functools.partial(_norm_matmul_kernel, col_chunk=col_chunk),
        grid=(t // tm,),
        in_specs=[pl.BlockSpec((tm, d), lambda i: (i, 0)),
                  pl.BlockSpec((1, d), lambda i: (0, 0)),
                  pl.BlockSpec((d, n), lambda i: (0, 0))],
        out_specs=pl.BlockSpec((tm, n), lambda i: (i, 0)),
        out_shape=jax.ShapeDtypeStruct((t, n), F32),
        compiler_params=_params("parallel"),
        name="norm_matmul",
    )(x, g.reshape(1, d), w)


def _merge_kernel(x_ref, ya_ref, yb_ref, yc_ref, g_ref, wg_ref, bg_ref, pa_ref, pb_ref, pc_ref, wo_ref, o_ref):
    x = x_ref[...]
    hb = _rms(x, g_ref[...]).astype(BF16)
    d = x.shape[1]
    merged = None
    for i, (y_ref, p_ref) in enumerate(((ya_ref, pa_ref), (yb_ref, pb_ref), (yc_ref, pc_ref))):
        gate = jax.nn.sigmoid(_dot(hb, wg_ref[:, i * d:(i + 1) * d]) + bg_ref[:, i * d:(i + 1) * d])
        term = gate * _dot(y_ref[...], p_ref[...])
        merged = term if merged is None else merged + term
    o_ref[...] = x + _dot(merged.astype(BF16), wo_ref[...])


def merge(x, ya, yb, yc, g, w_gate, b_gate, pa, pb, pc, w_out, tm=256):
    t, d = x.shape
    row = lambda i: (i, 0)
    fixed = lambda i: (0, 0)
    return pl.pallas_call(
        _merge_kernel,
        grid=(t // tm,),
        in_specs=[pl.BlockSpec((tm, d), row), pl.BlockSpec((tm, d), row), pl.BlockSpec((tm, d), row),
                  pl.BlockSpec((tm, d), row), pl.BlockSpec((1, d), fixed),
                  pl.BlockSpec((d, 3 * d), fixed), pl.BlockSpec((1, 3 * d), fixed),
                  pl.BlockSpec((d, d), fixed), pl.BlockSpec((d, d), fixed), pl.BlockSpec((d, d), fixed),
                  pl.BlockSpec((d, d), fixed)],
        out_specs=pl.BlockSpec((tm, d), row),
        out_shape=jax.ShapeDtypeStruct((t, d), F32),
        compiler_params=_params("parallel"),
        name="merge",
    )(x, ya, yb, yc, g.reshape(1, d), w_gate, b_gate.reshape(1, 3 * d), pa, pb, pc, w_out)


def _ssd_kernel(xbc_ref, z_ref, dt_ref, cw_ref, cb_ref, dtb_ref, a_ref, dsk_ref, ng_ref, rexp_ref, tri_ref,
                o_ref, buf_ref, state_ref):
    lc = SSD_CHUNK
    c = pl.program_id(1)

    @pl.when(c == 0)
    def _():
        buf_ref[0:8, :] = jnp.zeros((8, SSD_XBC), F32)
        state_ref[...] = jnp.zeros_like(state_ref)

    cur = xbc_ref[...]
    buf_ref[8:8 + lc, :] = cur
    acc = cb_ref[...] + cw_ref[0:1, :] * buf_ref[5:5 + lc, :]
    for k in range(1, SSD_CONV):
        acc = acc + cw_ref[k:k + 1, :] * buf_ref[5 + k:5 + k + lc, :]
    buf_ref[0:8, :] = cur[lc - 8:lc, :]
    xbc = _silu(acc)
    xs = xbc[:, :SSD_INNER]

    dt = jax.nn.softplus(dt_ref[...] + dtb_ref[...])
    acs = _dot_xf(tri_ref[...], dt * a_ref[...])
    acs_t = acs.T
    rexp = rexp_ref[...]
    dt_x = _dot_fx(dt, rexp)
    e_x = _dot_fx(jnp.exp(acs), rexp)
    dte_x = _dot_fx(jnp.exp(acs[lc - 1:lc, :] - acs), rexp)
    xdt = xs * dt_x
    xdt_b = xdt.astype(BF16)
    xw_b = (xdt * dte_x).astype(BF16)
    chunk_decay = e_x[lc - 1:lc, :]

    row = lax.broadcasted_iota(jnp.int32, (lc, lc), 0)
    col = lax.broadcasted_iota(jnp.int32, (lc, lc), 1)
    causal = row >= col
    gw = SSD_INNER // SSD_GROUPS
    hpg = SSD_HEADS // SSD_GROUPS
    n = SSD_STATE
    ys = []
    for g in range(SSD_GROUPS):
        b_g = xbc[:, SSD_INNER + g * n:SSD_INNER + (g + 1) * n].astype(BF16)
        c_g = xbc[:, SSD_INNER + (SSD_GROUPS + g) * n:SSD_INNER + (SSD_GROUPS + g + 1) * n].astype(BF16)
        cb = _dot_nt(c_g, b_g)
        s_in = state_ref[:, g * gw:(g + 1) * gw]
        y_off = _dot(c_g, s_in.astype(BF16)) * e_x[:, g * gw:(g + 1) * gw]
        state_ref[:, g * gw:(g + 1) * gw] = (chunk_decay[:, g * gw:(g + 1) * gw] * s_in
                                             + _dot_tn(b_g, xw_b[:, g * gw:(g + 1) * gw]))
        diag = []
        for e in range(hpg):
            h = g * hpg + e
            seg = jnp.exp(jnp.where(causal, acs[:, h:h + 1] - acs_t[h:h + 1, :], NEG))
            m = (cb * seg).astype(BF16)
            diag.append(_dot(m, xdt_b[:, h * SSD_HEAD_DIM:(h + 1) * SSD_HEAD_DIM]))
        ys.append(jnp.concatenate(diag, axis=-1) + y_off)
    y = jnp.concatenate(ys, axis=-1) + xs * dsk_ref[...]
    y = y * _silu(z_ref[...])
    for g in range(SSD_GROUPS):
        yg = y[:, g * gw:(g + 1) * gw]
        yg = yg * lax.rsqrt(jnp.mean(yg * yg, axis=-1, keepdims=True) + NORM_EPS)
        o_ref[:, g * gw:(g + 1) * gw] = (yg * ng_ref[:, g * gw:(g + 1) * gw]).astype(o_ref.dtype)


def ssd(proj, bsz, slen, conv_w, conv_b, dt_bias, a_log, d_skip, norm_g):
    lc = SSD_CHUNK
    nc = slen // lc
    t = bsz * slen
    pad = DT_PAD - SSD_HEADS
    dtb = jnp.pad(dt_bias.astype(F32), (0, pad)).reshape(1, DT_PAD)
    a = jnp.pad(-jnp.exp(a_log.astype(F32)), (0, pad)).reshape(1, DT_PAD)
    dsk = jnp.repeat(d_skip.astype(F32), SSD_HEAD_DIM).reshape(1, SSD_INNER)
    rexp = jnp.asarray((np.arange(DT_PAD)[:, None] == np.arange(SSD_INNER)[None, :] // SSD_HEAD_DIM)
                       .astype(np.float32), dtype=BF16)
    tri = jnp.asarray(np.tril(np.ones((lc, lc), np.float32)), dtype=BF16)
    rowmap = lambda b, c: (b * nc + c, 0)
    fixed = lambda b, c: (0, 0)
    zblk = SSD_XBC // SSD_INNER
    dtblk = (SSD_XBC + SSD_INNER) // DT_PAD
    return pl.pallas_call(
        _ssd_kernel,
        grid=(bsz, nc),
        in_specs=[pl.BlockSpec((lc, SSD_XBC), rowmap),
                  pl.BlockSpec((lc, SSD_INNER), lambda b, c: (b * nc + c, zblk)),
                  pl.BlockSpec((lc, DT_PAD), lambda b, c: (b * nc + c, dtblk)),
                  pl.BlockSpec((SSD_CONV, SSD_XBC), fixed), pl.BlockSpec((1, SSD_XBC), fixed),
                  pl.BlockSpec((1, DT_PAD), fixed), pl.BlockSpec((1, DT_PAD), fixed),
                  pl.BlockSpec((1, SSD_INNER), fixed), pl.BlockSpec((1, SSD_INNER), fixed),
                  pl.BlockSpec((DT_PAD, SSD_INNER), fixed), pl.BlockSpec((lc, lc), fixed)],
        out_specs=pl.BlockSpec((lc, SSD_INNER), rowmap),
        out_shape=jax.ShapeDtypeStruct((t, SSD_INNER), BF16),
        scratch_shapes=[pltpu.VMEM((lc + 8, SSD_XBC), F32), pltpu.VMEM((SSD_STATE, SSD_INNER), F32)],
        compiler_params=_params("parallel", "arbitrary"),
        name="ssd",
    )(proj, proj, proj, conv_w.astype(F32), conv_b.astype(F32).reshape(1, SSD_XBC), dtb, a, dsk,
      norm_g.astype(F32).reshape(1, SSD_INNER), rexp, tri)


def _ssd_weight(w_in_l):
    w = w_in_l[:, :IN_SSD]
    z, xbc, dt = w[:, :SSD_INNER], w[:, SSD_INNER:SSD_INNER + SSD_XBC], w[:, SSD_INNER + SSD_XBC:]
    dt = jnp.pad(dt, ((0, 0), (0, DT_PAD - SSD_HEADS)))
    return jnp.concatenate([xbc, z, dt], axis=1).astype(BF16)


def _t5_bucket(dist):
    max_exact = REL_BUCKETS // 2
    d = jnp.maximum(dist, 0)
    df = jnp.maximum(d, 1).astype(F32)
    large = max_exact + (jnp.log(df / max_exact) / math.log(REL_MAX_DIST / max_exact)
                         * (REL_BUCKETS - max_exact)).astype(jnp.int32)
    large = jnp.minimum(large, REL_BUCKETS - 1)
    return jnp.where(d < max_exact, d, large)


def _bias_table_kernel(bias_ref, bucket_ref, o_ref):
    h = pl.program_id(0)
    for d in range(MOBA_NEAR):
        bk = bucket_ref[d]
        acc = jnp.where(bk < 0, NEG, 0.0)
        for b in range(REL_BUCKETS):
            acc = jnp.where(bk == b, bias_ref[h, b] * LOG2E, acc)
        o_ref[d] = acc
    o_ref[MOBA_NEAR] = jnp.zeros(o_ref.shape[1:], F32)


def moba_bias_table(rel_bias):
    blk = MOBA_BLOCK
    q = jnp.arange(blk, dtype=jnp.int32)[None, :, None]
    k = jnp.arange(blk, dtype=jnp.int32)[None, None, :]
    dist = jnp.arange(MOBA_NEAR, dtype=jnp.int32)[:, None, None] * blk + q - k
    bucket = jnp.where(dist < 0, -1, _t5_bucket(dist))
    return pl.pallas_call(
        _bias_table_kernel,
        grid=(MOBA_HEADS,),
        in_specs=[pl.BlockSpec(memory_space=pltpu.SMEM),
                  pl.BlockSpec((MOBA_NEAR, blk, blk), lambda h: (0, 0, 0))],
        out_specs=pl.BlockSpec((None, MOBA_NEAR + 1, blk, blk), lambda h: (h, 0, 0, 0)),
        out_shape=jax.ShapeDtypeStruct((MOBA_HEADS, MOBA_NEAR + 1, blk, blk), F32),
        compiler_params=_params("arbitrary"),
        name="moba_bias_table",
    )(rel_bias.astype(F32).T, bucket)


def _moba_prep_kernel(q_ref, k_ref, v_ref, qg_ref, kg_ref, far_ref, qa_ref, kb_ref, vb_ref, kmean_ref):
    i = pl.program_id(1)
    nchain, blk, dh = q_ref.shape
    chains = range(nchain)
    half = dh // 2

    @pl.when(i == 0)
    def _():
        kmean_ref[...] = jnp.zeros_like(kmean_ref)

    qn = [_rms(q_ref[c], qg_ref[...]) for c in chains]
    kn = [_rms(k_ref[c], kg_ref[...]) for c in chains]
    gate = [_dot_nt(qn[c], kmean_ref[c], HIGHEST) for c in chains]
    col = lax.broadcasted_iota(jnp.int32, (blk, dh), 1)
    colf = col.astype(F32)
    valid = col < i
    g = [jnp.where(valid, x, NEG) for x in gate]
    sel = [col < 0 for _ in chains]
    for _ in range(MOBA_TOPK):
        m = [jnp.max(x, axis=-1, keepdims=True) for x in g]
        idx = [jnp.min(jnp.where(x == mx, colf, float(dh)), axis=-1, keepdims=True) for x, mx in zip(g, m)]
        hit = [colf == ix for ix in idx]
        sel = [sl | ht for sl, ht in zip(sel, hit)]
        g = [jnp.where(ht, -jnp.inf, x) for ht, x in zip(hit, g)]

    far = far_ref[...] * LOG2E
    far_hi = far.astype(BF16).astype(F32)
    far_lo = far - far_hi
    low = col >= half
    jcol = jnp.where(low, col - half, col)
    is_far = (i - jcol) >= MOBA_NEAR
    for c in chains:
        sel_f = jnp.where(sel[c] & valid, 1.0, 0.0)
        sel2 = jnp.where(low, pltpu.roll(sel_f, half, 1), sel_f) > 0.5
        pen_hi = jnp.where(jcol == i, 0.0, jnp.where(sel2, jnp.where(is_far, far_hi, 0.0), NEG))
        pen_lo = jnp.where(sel2 & is_far, far_lo, 0.0)
        qa_ref[c, :, 0:dh] = (qn[c] * (dh ** -0.5 * LOG2E)).astype(BF16)
        qa_ref[c, :, dh:2 * dh] = jnp.where(low, pen_lo, pen_hi).astype(BF16)
        kb_ref[c] = kn[c].astype(BF16)
        vb_ref[c] = v_ref[c].astype(BF16)
        kmean_ref[c, pl.ds(i, 1), :] = jnp.mean(kn[c], axis=0, keepdims=True)


def _moba_attn_kernel(qa_ref, kb_ref, vb_ref, tab_ref, o_ref, m_ref, acc_ref):
    a = pl.program_id(2)
    blk = MOBA_BLOCK
    dh = m_ref.shape[-1]
    chains = [(bi, hi) for bi in range(qa_ref.shape[0]) for hi in range(qa_ref.shape[1])]
    q = [qa_ref[bi, hi] for bi, hi in chains]
    m_ref[...] = jnp.full_like(m_ref, 0.1 * NEG)
    acc_ref[...] = jnp.zeros_like(acc_ref)

    def step(off, nk, tabs):
        row = lax.broadcasted_iota(jnp.int32, (nk, dh), 0)
        lane = lax.broadcasted_iota(jnp.int32, (nk, dh), 1)
        key_block = off // blk + lax.shift_right_logical(row, blk.bit_length() - 1)
        onehot = jnp.where((lane & (dh // 2 - 1)) == key_block, 1.0, 0.0).astype(BF16)
        ones = jnp.ones((nk, dh), BF16)
        s = [_dot_nt(q[n], jnp.concatenate([kb_ref[bi, hi, pl.ds(off, nk), :], onehot], axis=1))
             for n, (bi, hi) in enumerate(chains)]
        if tabs is not None:
            s = [x + tabs[hi] for x, (_, hi) in zip(s, chains)]
        m_old = [m_ref[n] for n in range(len(chains))]
        m_new = [jnp.maximum(mo, jnp.max(x, axis=-1, keepdims=True)) for mo, x in zip(m_old, s)]
        alpha = [jnp.exp2(mo - mn) for mo, mn in zip(m_old, m_new)]
        p = [jnp.exp2(x - jnp.concatenate([mn] * (nk // dh), axis=-1)).astype(BF16) for x, mn in zip(s, m_new)]
        pv = [_dot(p[n], jnp.concatenate([vb_ref[bi, hi, pl.ds(off, nk), :], ones], axis=1))
              for n, (bi, hi) in enumerate(chains)]
        for n in range(len(chains)):
            acc_ref[n] = jnp.concatenate([alpha[n], alpha[n]], axis=-1) * acc_ref[n] + pv[n]
            m_ref[n] = m_new[n]

    first = MOBA_QBLOCKS * a
    n_far = jnp.maximum(first - (MOBA_NEAR - 1), 0)

    def far_body(t, c):
        step(pl.multiple_of(t * (MOBA_QBLOCKS * blk), MOBA_QBLOCKS * blk), MOBA_QBLOCKS * blk, None)
        return c

    def near_body(u, c):
        j0 = n_far + MOBA_QBLOCKS * u
        tabs = [jnp.concatenate(
            [jnp.concatenate([tab_ref[hi, jnp.clip(first + r - (j0 + kb), 0, MOBA_NEAR)]
                              for kb in range(MOBA_QBLOCKS)], axis=1) for r in range(MOBA_QBLOCKS)], axis=0)
            for hi in range(qa_ref.shape[1])]
        step(pl.multiple_of(j0 * blk, MOBA_QBLOCKS * blk), MOBA_QBLOCKS * blk, tabs)
        return c

    lax.fori_loop(0, n_far // MOBA_QBLOCKS, far_body, 0)
    lax.fori_loop(0, (first + MOBA_QBLOCKS - n_far) // MOBA_QBLOCKS, near_body, 0)
    for n, (bi, hi) in enumerate(chains):
        acc = acc_ref[n]
        o_ref[bi, :, hi * dh:(hi + 1) * dh] = (acc[:, :dh] / acc[:, dh:]).astype(o_ref.dtype)


def moba(proj, bsz, slen, q_norm_g, k_norm_g, bias_table, far_bias):
    blk, dh, nh = MOBA_BLOCK, MOBA_HEAD_DIM, MOBA_HEADS
    nb = slen // blk
    tq = MOBA_QBLOCKS * blk
    assert slen % tq == 0 and nb <= dh // 2 and (MOBA_NEAR - 1) % MOBA_QBLOCKS == 0
    t = bsz * slen
    far = jnp.broadcast_to(far_bias.astype(F32)[:, None, None], (nh, 1, dh))
    per_block = lambda width: pl.BlockSpec((bsz, None, blk, width), lambda h, i: (0, h, i, 0))
    proj3 = proj.reshape(bsz, slen, 3 * nh * dh)
    qa, kb, vb = pl.pallas_call(
        _moba_prep_kernel,
        grid=(nh, nb),
        in_specs=[pl.BlockSpec((bsz, blk, dh), lambda h, i: (0, i, h)),
                  pl.BlockSpec((bsz, blk, dh), lambda h, i: (0, i, nh + h)),
                  pl.BlockSpec((bsz, blk, dh), lambda h, i: (0, i, 2 * nh + h)),
                  pl.BlockSpec((1, dh), lambda h, i: (0, 0)),
                  pl.BlockSpec((1, dh), lambda h, i: (0, 0)),
                  pl.BlockSpec((None, 1, dh), lambda h, i: (h, 0, 0))],
        out_specs=[per_block(2 * dh), per_block(dh), per_block(dh)],
        out_shape=[jax.ShapeDtypeStruct((bsz, nh, slen, 2 * dh), BF16),
                   jax.ShapeDtypeStruct((bsz, nh, slen, dh), BF16),
                   jax.ShapeDtypeStruct((bsz, nh, slen, dh), BF16)],
        scratch_shapes=[pltpu.VMEM((bsz, dh, dh), F32)],
        compiler_params=_params("parallel", "arbitrary"),
        name="moba_prep",
    )(proj3, proj3, proj3, q_norm_g.astype(F32).reshape(1, dh), k_norm_g.astype(F32).reshape(1, dh), far)
    cb, chd = MOBA_CHAIN_BATCH, MOBA_CHAIN_HEADS
    assert bsz % cb == 0 and nh % chd == 0
    whole = pl.BlockSpec((cb, chd, slen, dh), lambda b, h, a: (b, h, 0, 0), pipeline_mode=pl.Buffered(1))
    out = pl.pallas_call(
        _moba_attn_kernel,
        grid=(bsz // cb, nh // chd, slen // tq),
        in_specs=[pl.BlockSpec((cb, chd, tq, 2 * dh), lambda b, h, a: (b, h, a, 0)), whole, whole,
                  pl.BlockSpec((chd, MOBA_NEAR + 1, blk, blk), lambda b, h, a: (h, 0, 0, 0))],
        out_specs=pl.BlockSpec((cb, tq, chd * dh), lambda b, h, a: (b, a, h)),
        out_shape=jax.ShapeDtypeStruct((bsz, slen, nh * dh), BF16),
        scratch_shapes=[pltpu.VMEM((cb * chd, tq, dh), F32), pltpu.VMEM((cb * chd, tq, 2 * dh), F32)],
        compiler_params=_params("parallel", "parallel", "arbitrary"),
        name="moba_attn",
    )(qa, kb, vb, bias_table)
    return out.reshape(t, nh * dh)


def _rwkv_prep_kernel(cur_ref, prev_ref, mu_ref, w0_ref, ww2_ref, a0_ref, wa2_ref, wg2_ref, kk_ref, ka_ref,
                      hsum_ref, hexp_ref, r_ref, lw_ref, k_ref, v_ref, a_ref, b_ref, g_ref, buf_ref,
                      *, tiles_per_seq):
    i = pl.program_id(0)
    tm = cur_ref.shape[0]
    c = RWKV_INNER
    cur = cur_ref[...]
    first = (i % tiles_per_seq) == 0
    buf_ref[8:8 + tm, :] = cur
    buf_ref[0:8, :] = jnp.where(first, 0.0, prev_ref[...])
    prev = buf_ref[7:7 + tm, :]
    xs = cur + (prev - cur) * mu_ref[...]
    r, k, v = xs[:, :c], xs[:, c:2 * c], xs[:, 2 * c:3 * c]
    o1 = 3 * c + RWKV_DECAY_LORA
    o2 = o1 + RWKV_ICLR_LORA
    wl, al, gl = xs[:, 3 * c:o1], xs[:, o1:o2], xs[:, o2:]
    w = -jax.nn.softplus(-(w0_ref[...] + _dot_ff(jnp.tanh(wl), ww2_ref[...]))) - 0.5
    a = jax.nn.sigmoid(a0_ref[...] + _dot_ff(al, wa2_ref[...]))
    kk = k * kk_ref[...]
    ss = _dot_fx(_dot_fx(kk * kk, hsum_ref[...]), hexp_ref[...])
    kk = kk * lax.rsqrt(jnp.maximum(ss, 1e-12))
    r_ref[...] = r
    lw_ref[...] = -jnp.exp(w)
    k_ref[...] = k * (1.0 + (a - 1.0) * ka_ref[...])
    v_ref[...] = v
    a_ref[...] = -kk
    b_ref[...] = kk * a
    g_ref[...] = _dot_ff(jax.nn.sigmoid(gl), wg2_ref[...])


def _bd_rows(y, mask):
    return jnp.where(mask, jnp.concatenate([y] * RWKV_GROUP, axis=0), 0.0)


def _rwkv_scan_kernel(r_ref, lw_ref, k_ref, v_ref, a_ref, b_ref, g_ref, rk_ref, lng_ref, lnb_ref,
                      tri_ref, lows_ref, lowi_ref, eye_ref, bd_ref, avg_ref, o_ref, s_ref, *, mxu_dtype):
    ci = pl.program_id(1)
    ch = RWKV_CHUNK

    @pl.when(ci == 0)
    def _():
        s_ref[...] = jnp.zeros_like(s_ref)

    prec = HIGHEST if mxu_dtype == F32 else None

    def mm(x, y):
        return _dot(x.astype(mxu_dtype), y.astype(mxu_dtype), prec)

    def mm_nt(x, y):
        return _dot_nt(x.astype(mxu_dtype), y.astype(mxu_dtype), prec)

    def mm_tn(x, y):
        return _dot_tn(x.astype(mxu_dtype), y.astype(mxu_dtype), prec)

    bd = bd_ref[...] > 0.5
    strict = lows_ref[...] > 0.5
    incl = lowi_ref[...] > 0.5
    gw = bd_ref.shape[0]
    eye_big = lax.broadcasted_iota(jnp.int32, (gw, gw), 0) == lax.broadcasted_iota(jnp.int32, (gw, gw), 1)
    avg = avg_ref[...]

    bsz = r_ref.shape[0]
    chains = [(bi, gi) for bi in range(bsz) for gi in range(r_ref.shape[2] // gw)]
    cols = [slice(gi * gw, (gi + 1) * gw) for _, gi in chains]

    def each(fn, *lists):
        return [fn(*xs) for xs in zip(*lists)]

    def bdr(ys):
        return [_bd_rows(y, bd) for y in ys]

    r, lw, k, v, a, b = ([ref[bi, :, c] for (bi, _), c in zip(chains, cols)]
                         for ref in (r_ref, lw_ref, k_ref, v_ref, a_ref, b_ref))
    cl = each(lambda x: _dot_xf(tri_ref[...], x), lw)
    cl_end = [x[ch - 1:ch, :] for x in cl]
    e_neg = [jnp.exp(-x) for x in cl]
    a_t = each(lambda x, c, l: x * jnp.exp(c - l), a, cl, lw)
    r_t = each(lambda x, c: x * jnp.exp(c), r, cl)
    b_t = each(jnp.multiply, b, e_neg)
    k_t = each(jnp.multiply, k, e_neg)
    e_end = each(lambda ce, c: jnp.exp(ce - c), cl_end, cl)
    b_h = each(jnp.multiply, b, e_end)
    k_h = each(jnp.multiply, k, e_end)

    ar = each(lambda x, y: jnp.concatenate([x, y], axis=0), a_t, r_t)
    gb = each(mm_nt, ar, bdr(b_t))
    gk = each(mm_nt, ar, bdr(k_t))
    a_ab = [jnp.where(strict, x[:ch], 0.0) for x in gb]
    a_ak = [jnp.where(strict, x[:ch], 0.0) for x in gk]
    q_rb = [jnp.where(incl, x[ch:], 0.0) for x in gb]
    q_rk = [jnp.where(incl, x[ch:], 0.0) for x in gk]

    t_inv = [eye_ref[...] + x for x in a_ab]
    p = a_ab
    for _ in range(int(math.log2(ch)) - 1):
        p = each(mm, p, bdr(p))
        t_inv = each(lambda t, tp: t + tp, t_inv, each(mm, t_inv, bdr(p)))

    vbd = bdr(v)
    w_t = each(mm, t_inv, bdr(a_t))
    u0 = each(mm, t_inv, bdr(each(mm, a_ak, vbd)))
    z = each(lambda x, y: x + y, r_t, each(mm, q_rb, bdr(w_t)))
    y_loc = each(lambda x, y: x + y, each(mm, q_rb, bdr(u0)), each(mm, q_rk, vbd))
    bwu = each(lambda x, w, u: mm_tn(x, jnp.concatenate([w, u], axis=1)), b_h, w_t, u0)
    kv = each(mm_tn, k_h, v)
    m_s = each(lambda x, ce: jnp.where(bd, x[:, :gw], 0.0) + jnp.where(eye_big, jnp.exp(ce), 0.0), bwu, cl_end)
    n_s = each(lambda x, y: jnp.where(bd, x[:, gw:] + y, 0.0), bwu, kv)
    s0 = [s_ref[n] for n in range(len(chains))]
    zs = each(lambda x, m, s: mm(jnp.concatenate([x, m], axis=0), s), z, m_s, s0)
    y = each(lambda x, yl: x[:ch] + yl, zs, y_loc)
    for n in range(len(chains)):
        s_ref[n] = zs[n][ch:] + n_s[n]

    mean = [_dot_fx(x, avg) for x in y]
    dlt = each(lambda x, m: x - m, y, mean)
    var = [_dot_fx(x * x, avg) for x in dlt]
    rkk = each(lambda x, kk, c: _dot_fx(x * kk * rk_ref[:, c], avg), r, k, cols)
    for n, ((bi, _), c) in enumerate(zip(chains, cols)):
        yn = dlt[n] * lax.rsqrt(var[n] + RWKV_LN_EPS) * lng_ref[:, c] + lnb_ref[:, c]
        bonus = rkk[n] * float(RWKV_HEAD_DIM) * v[n]
        o_ref[bi, :, c] = ((yn + bonus) * g_ref[bi, :, c]).astype(o_ref.dtype)


def rwkv(proj, bsz, slen, mu, w0, w_w2, a0, w_a2, w_g2, k_k, k_a, r_k, lnx_g, lnx_b, tm=256, mxu_dtype=BF16,
         groups_per_step=4):
    c, ch, grp, n = RWKV_INNER, RWKV_CHUNK, RWKV_GROUP, RWKV_HEAD_DIM
    t = bsz * slen
    f = lambda p: p.astype(F32).reshape(1, -1)
    heads = np.arange(c) // n
    hsum = jnp.asarray((heads[:, None] == np.arange(128)[None, :]).astype(np.float32), dtype=BF16)
    hexp = hsum.T
    row = lambda i: (i, 0)
    fixed = lambda i: (0, 0)
    wide = pl.BlockSpec((tm, c), row)
    vec = pl.BlockSpec((1, c), fixed)
    r, lw, k, v, a, b, g = pl.pallas_call(
        functools.partial(_rwkv_prep_kernel, tiles_per_seq=slen // tm),
        grid=(t // tm,),
        in_specs=[pl.BlockSpec((tm, RWKV_IN), row),
                  pl.BlockSpec((8, RWKV_IN), lambda i: (jnp.maximum(i * (tm // 8) - 1, 0), 0)),
                  pl.BlockSpec((1, RWKV_IN), fixed), vec,
                  pl.BlockSpec((RWKV_DECAY_LORA, c), fixed), vec,
                  pl.BlockSpec((RWKV_ICLR_LORA, c), fixed), pl.BlockSpec((RWKV_GATE_LORA, c), fixed),
                  vec, vec, pl.BlockSpec((c, 128), fixed), pl.BlockSpec((128, c), fixed)],
        out_specs=[wide] * 7,
        out_shape=[jax.ShapeDtypeStruct((t, c), F32)] * 7,
        scratch_shapes=[pltpu.VMEM((tm + 8, RWKV_IN), F32)],
        compiler_params=_params("parallel"),
        name="rwkv_prep",
    )(proj, proj, f(mu), f(w0), w_w2.astype(F32), f(a0), w_a2.astype(F32), w_g2.astype(F32), f(k_k), f(k_a),
      hsum, hexp)

    gw = grp * n
    nc = slen // ch
    tt = np.arange(ch)
    colt = np.arange(gw) % ch
    hrow = np.arange(gw) // n
    tri = jnp.asarray((tt[:, None] >= tt[None, :]).astype(np.float32), dtype=BF16)
    lows = jnp.asarray((colt[None, :] < tt[:, None]).astype(np.float32))
    lowi = jnp.asarray((colt[None, :] <= tt[:, None]).astype(np.float32))
    eye = jnp.asarray((colt[None, :] == tt[:, None]).astype(np.float32))
    bdm = (hrow[:, None] == hrow[None, :]).astype(np.float32)
    bd = jnp.asarray(bdm)
    avg = jnp.asarray(bdm / n, dtype=BF16)
    sw = groups_per_step * gw
    blk = pl.BlockSpec((bsz, ch, sw), lambda gi, ci: (0, ci, gi))
    gvec = pl.BlockSpec((1, sw), lambda gi, ci: (0, gi))
    const = lambda shape: pl.BlockSpec(shape, lambda gi, ci: (0, 0))
    seq = lambda x: x.reshape(bsz, slen, c)
    out = pl.pallas_call(
        functools.partial(_rwkv_scan_kernel, mxu_dtype=mxu_dtype),
        grid=(c // sw, nc),
        in_specs=[blk] * 7 + [gvec] * 3 + [const((ch, ch)), const((ch, gw)), const((ch, gw)), const((ch, gw)),
                                           const((gw, gw)), const((gw, gw))],
        out_specs=blk,
        out_shape=jax.ShapeDtypeStruct((bsz, slen, c), BF16),
        scratch_shapes=[pltpu.VMEM((bsz * groups_per_step, gw, gw), F32)],
        compiler_params=_params("parallel", "arbitrary"),
        name="rwkv_scan",
    )(seq(r), seq(lw), seq(k), seq(v), seq(a), seq(b), seq(g), f(r_k), f(lnx_g), f(lnx_b),
      tri, lows, lowi, eye, bd, avg)
    return out.reshape(t, c)


PEER_RANKS = PEER_TOPK
PEER_PAIRS = [(p, q) for p in range(PEER_RANKS) for q in range(PEER_RANKS // (p + 1))]
PEER_CAND_ROWS = -(-len(PEER_PAIRS) // 8) * 8
PEER_TOP_ROWS = -(-2 * PEER_RANKS // 8) * 8


def _peer_kernel(x_ref, g_ref, wq_ref, k1_ref, k2_ref, u_ref, vt_ref, o_ref,
                 h2_ref, cnt_ref, e1_ref, r2_ref, e2_ref, acc_ref, top_ref, cand_ref, sel_ref,
                 hu0_ref, hu1_ref, p0_ref, p1_ref):
    e = pl.program_id(1)
    n_blocks = 2 * (pl.num_programs(1) - 1)
    nk = PEER_KEYS
    half = PEER_KEY_DIM // 2
    eb = u_ref.shape[0] // 2

    @pl.when(e == 0)
    def _():
        for ref in (hu0_ref, hu1_ref, p0_ref, p1_ref):
            ref[...] = jnp.zeros_like(ref)
        hb = _rms(x_ref[...], g_ref[...]).astype(BF16)
        h2_ref[...] = hb
        q = _dot(hb, wq_ref[...])
        cand_ref[...] = jnp.full_like(cand_ref, -jnp.inf)
        for h in range(PEER_HEADS):
            base = h * PEER_KEY_DIM
            s1 = _dot_ff(k1_ref[...], q[:, base:base + half], nt=True)
            s2 = _dot_ff(k2_ref[...], q[:, base + half:base + 2 * half], nt=True)
            ranks = []
            for side, s in enumerate((s1, s2)):
                cur = s
                rank = jnp.full(s.shape, float(PEER_KEYS), F32)
                for rnk in range(PEER_RANKS):
                    m = jnp.max(cur, axis=0, keepdims=True)
                    top_ref[side * PEER_RANKS + rnk:side * PEER_RANKS + rnk + 1, :] = m
                    hit = cur == m
                    rank = jnp.where(hit, float(rnk), rank)
                    cur = jnp.where(hit, -jnp.inf, cur)
                ranks.append(rank)
            for ci, (p, qq) in enumerate(PEER_PAIRS):
                cand_ref[ci:ci + 1, :] = top_ref[p:p + 1, :] + top_ref[PEER_RANKS + qq:PEER_RANKS + qq + 1, :]
            cand = cand_ref[...]
            cur = cand
            best = jnp.max(cur, axis=0, keepdims=True)
            tau = best
            for _ in range(PEER_TOPK - 1):
                cur = jnp.where(cur == tau, -jnp.inf, cur)
                tau = jnp.max(cur, axis=0, keepdims=True)
            chosen = cand >= tau
            zsum = jnp.sum(jnp.where(chosen, jnp.exp(cand - best), 0.0), axis=0, keepdims=True)
            sel_ref[...] = jnp.where(chosen, 1.0, 0.0)
            rank1 = ranks[0].astype(BF16)
            cnt = jnp.zeros(s1.shape, BF16)
            row0 = 0
            for p in range(PEER_RANKS):
                nq = PEER_RANKS // (p + 1)
                n_p = jnp.sum(sel_ref[row0:row0 + nq, :], axis=0, keepdims=True).astype(BF16)
                cnt = jnp.where(rank1 == jnp.asarray(p, BF16), n_p, cnt)
                row0 += nq
            cnt_ref[h] = cnt.astype(F32)
            e1_ref[h] = jnp.exp(s1 - top_ref[0:1, :]) / zsum
            r2_ref[h] = ranks[1].astype(BF16)
            e2_ref[h] = jnp.exp(s2 - top_ref[PEER_RANKS:PEER_RANKS + 1, :]).astype(BF16)
        acc_ref[...] = jnp.zeros_like(acc_ref)

    hu_slots, p_slots = (hu0_ref, hu1_ref), (p0_ref, p1_ref)
    tt = h2_ref.shape[0]
    d = acc_ref.shape[0]
    tok = [slice(n * (tt // 2), (n + 1) * (tt // 2)) for n in range(2)]
    for par in range(2):
        tick = 2 * e + par
        blk_b = jnp.clip(tick - 1, 0, n_blocks - 1)
        p_rd, p_wr, hu_rd, hu_wr = p_slots[par], p_slots[1 - par], hu_slots[1 - par], hu_slots[par]

        def down(n, r):
            rows = slice(r * (d // 2), (r + 1) * (d // 2))
            acc_ref[rows, tok[n]] += _dot(vt_ref[rows, par * eb:(par + 1) * eb], p_rd[:, tok[n]])

        def up(n, r):
            rows = slice(r * (eb // 2), (r + 1) * (eb // 2))
            hu_wr[rows, tok[n]] = _dot_nt(u_ref[par * eb + r * (eb // 2):par * eb + (r + 1) * (eb // 2), :],
                                          h2_ref[tok[n], :])

        def weigh(n, ib):
            rows = slice(ib * nk, (ib + 1) * nk)
            i = blk_b * (eb // nk) + ib
            hu = hu_rd[rows, tok[n]]
            act = (0.5 * hu * (1.0 + lax.erf(hu * (2.0 ** -0.5)))).astype(BF16)
            w = None
            for h in range(PEER_HEADS):
                cnt = cnt_ref[h, pl.ds(i, 1), tok[n]].astype(BF16)
                gate1 = e1_ref[h, pl.ds(i, 1), tok[n]].astype(BF16)
                term = jnp.where(r2_ref[h, :, tok[n]] < cnt, e2_ref[h, :, tok[n]] * gate1, jnp.zeros((), BF16))
                w = term if w is None else w + term
            p_wr[rows, tok[n]] = w * act

        matmul_jobs = [(fn, n, r) for n in range(2) for r in range(2) for fn in (down, up)]
        vector_jobs = [(n, ib) for n in range(2) for ib in range(eb // nk)]
        assert len(matmul_jobs) == len(vector_jobs)
        for (fn, n, r), (vn, ib) in zip(matmul_jobs, vector_jobs):
            fn(n, r)
            weigh(vn, ib)

    @pl.when(e == pl.num_programs(1) - 1)
    def _():
        o_ref[...] = x_ref[...] + acc_ref[...].T


def peer(x, g, wq, k1, k2, u, vt, tt=512, eb=512):
    t, d = x.shape
    ne = u.shape[0]
    nh, nk = PEER_HEADS, PEER_KEYS
    nb2 = ne // (2 * eb)
    assert ne % (2 * eb) == 0
    return pl.pallas_call(
        _peer_kernel,
        grid=(t // tt, nb2 + 1),
        in_specs=[pl.BlockSpec((tt, d), lambda i, e: (i, 0)),
                  pl.BlockSpec((1, d), lambda i, e: (0, 0)),
                  pl.BlockSpec((d, nh * PEER_KEY_DIM), lambda i, e: (0, 0)),
                  pl.BlockSpec((nk, PEER_KEY_DIM // 2), lambda i, e: (0, 0)),
                  pl.BlockSpec((nk, PEER_KEY_DIM // 2), lambda i, e: (0, 0)),
                  pl.BlockSpec((2 * eb, d), lambda i, e: (jnp.minimum(e, nb2 - 1), 0)),
                  pl.BlockSpec((d, 2 * eb), lambda i, e: (0, jnp.maximum(e - 1, 0)))],
        out_specs=pl.BlockSpec((tt, d), lambda i, e: (i, 0)),
        out_shape=jax.ShapeDtypeStruct((t, d), F32),
        scratch_shapes=[pltpu.VMEM((tt, d), BF16),
                        pltpu.VMEM((nh, nk, tt), F32), pltpu.VMEM((nh, nk, tt), F32),
                        pltpu.VMEM((nh, nk, tt), BF16), pltpu.VMEM((nh, nk, tt), BF16),
                        pltpu.VMEM((d, tt), F32),
                        pltpu.VMEM((PEER_TOP_ROWS, tt), F32), pltpu.VMEM((PEER_CAND_ROWS, tt), F32),
                        pltpu.VMEM((PEER_CAND_ROWS, tt), F32),
                        pltpu.VMEM((eb, tt), F32), pltpu.VMEM((eb, tt), F32),
                        pltpu.VMEM((eb, tt), BF16), pltpu.VMEM((eb, tt), BF16)],
        compiler_params=_params("parallel", "arbitrary"),
        name="peer",
    )(x, g.astype(F32).reshape(1, d), wq, k1.astype(F32), k2.astype(F32), u, vt)


def kernel(x, rel_bias, norm1_g, w_in, conv_w, conv_b, dt_bias, a_log, d_skip, ssd_norm_g, q_norm_g, k_norm_g, rwkv_mu, w0, w_w2, a0, w_a2, w_g2, k_k, k_a, r_k, lnx_g, lnx_b, p_ssd, p_moba, p_rwkv, w_gate, b_gate, w_out, norm2_g, peer_wq, peer_k1, peer_k2, peer_u, peer_v):
    bsz, slen, d = x.shape
    t = bsz * slen
    xf = x.astype(F32).reshape(t, d)
    bias_table = moba_bias_table(rel_bias)
    far_bias = rel_bias[REL_BUCKETS - 1]
    for l in range(w_in.shape[0]):
        g1 = norm1_g[l].astype(F32)
        w_moba = w_in[l][:, IN_SSD:IN_SSD + IN_MOBA].astype(BF16)
        w_rwkv = w_in[l][:, IN_SSD + IN_MOBA:].astype(BF16)
        ya = ssd(norm_matmul(xf, g1, _ssd_weight(w_in[l])), bsz, slen, conv_w[l], conv_b[l], dt_bias[l],
                 a_log[l], d_skip[l], ssd_norm_g[l])
        yb = moba(norm_matmul(xf, g1, w_moba), bsz, slen, q_norm_g[l], k_norm_g[l], bias_table, far_bias)
        yc = rwkv(norm_matmul(xf, g1, w_rwkv), bsz, slen, rwkv_mu[l], w0[l], w_w2[l], a0[l], w_a2[l], w_g2[l],
                  k_k[l], k_a[l], r_k[l], lnx_g[l], lnx_b[l])
        xf = merge(xf, ya, yb, yc, g1, w_gate[l].astype(BF16), b_gate[l].astype(F32), p_ssd[l].astype(BF16),
                   p_moba[l].astype(BF16), p_rwkv[l].astype(BF16), w_out[l].astype(BF16))
        xf = peer(xf, norm2_g[l], peer_wq[l].astype(BF16), peer_k1[l], peer_k2[l], peer_u[l].astype(BF16),
                  peer_v[l].astype(BF16).T)
    return xf.reshape(bsz, slen, d).astype(x.dtype)
```
